```python
import jax, jax.numpy as jnp
from jax import lax
import numpy as np

D_MODEL = 1024
BATCH = 16
SEQ = 4096
DEPTH = 1

CTX_LEN = 256
GRID_W = 64
CONV_WIDTH = 512
HGRN_WIDTH = 512
MIX_WIDTH = CONV_WIDTH + HGRN_WIDTH
HGRN_HEAD_DIM = 128
HGRN_HEADS = HGRN_WIDTH // HGRN_HEAD_DIM
CONV_K = 31
CONV_PAD = (CONV_K - 1) // 2
CHUNK = 64
N_EXPERTS = 16
EXPERT_FF = D_MODEL
CAPACITY_FACTOR = 2
N_MOD = 6
EPS = 1e-6

OFF_CONV_A = 0
OFF_CONV_B = OFF_CONV_A + CONV_WIDTH
OFF_Q = OFF_CONV_B + CONV_WIDTH
OFF_I = OFF_Q + HGRN_WIDTH
OFF_FF = OFF_I + HGRN_WIDTH
OFF_FB = OFF_FF + HGRN_WIDTH
OFF_G = OFF_FB + HGRN_WIDTH
IN_COLS = OFF_G + HGRN_WIDTH

kernel_name = 'hymba_conformer_hgrn2_ecmoe_dit'


def rmsnorm(x, g):
    xf = x.astype(jnp.float32)
    y = xf * lax.rsqrt(jnp.mean(xf * xf, axis=-1, keepdims=True) + EPS)
    return (y * g.astype(jnp.float32)).astype(x.dtype)


def layernorm(x, g, b):
    xf = x.astype(jnp.float32)
    mu = jnp.mean(xf, axis=-1, keepdims=True)
    xc = xf - mu
    y = xc * lax.rsqrt(jnp.mean(xc * xc, axis=-1, keepdims=True) + EPS)
    return (y * g.astype(jnp.float32) + b.astype(jnp.float32)).astype(x.dtype)


def modulation(cvec, w, b):
    m = jnp.einsum('bd,de->be', jax.nn.silu(cvec), w) + b
    return [t[:, None, :] for t in jnp.split(m, N_MOD, axis=-1)]


def modulate(h, shift, scale):
    return h * (1 + scale) + shift


def dwconv_seq(u, taps):
    ch = u.shape[-1]
    return lax.conv_general_dilated(u, taps[:, None, :], window_strides=(1,),
                                    padding=[(CONV_PAD, CONV_PAD)],
                                    dimension_numbers=('NWC', 'WIO', 'NWC'),
                                    feature_group_count=ch)


def dwconv_grid(u, taps):
    b, n, ch = u.shape
    rows = n // GRID_W
    g = u.reshape(b, rows, GRID_W, ch)
    half = ch // 2
    dn = ('NHWC', 'HWIO', 'NHWC')
    horiz = lax.conv_general_dilated(g[..., :half], taps[None, :, None, :half], (1, 1),
                                     [(0, 0), (CONV_PAD, CONV_PAD)], dimension_numbers=dn,
                                     feature_group_count=half)
    vert = lax.conv_general_dilated(g[..., half:], taps[:, None, None, half:], (1, 1),
                                    [(CONV_PAD, CONV_PAD), (0, 0)], dimension_numbers=dn,
                                    feature_group_count=ch - half)
    return jnp.concatenate([horiz, vert], axis=-1).reshape(b, n, ch)


def conv_module(proj, taps, bias, ln_g, ln_b, conv_fn):
    u = proj[..., OFF_CONV_A:OFF_CONV_B] * jax.nn.sigmoid(proj[..., OFF_CONV_B:OFF_Q])
    u = conv_fn(u, taps) + bias
    return jax.nn.silu(layernorm(u, ln_g, ln_b))


def split_heads(t):
    b, n, _ = t.shape
    return t.reshape(b, n, HGRN_HEADS, HGRN_HEAD_DIM).transpose(0, 2, 1, 3)


def merge_heads(t):
    b, h, n, d = t.shape
    return t.transpose(0, 2, 1, 3).reshape(b, n, h * d)


def forget_gate(f_logits, lb):
    f = lb + (1.0 - lb) * jax.nn.sigmoid(f_logits.astype(jnp.float32))
    return split_heads(jnp.log(f)), split_heads(1.0 - f)


def gla_scan(k, v, log_f, s0, q=None):
    def to_chunks(t):
        b, h, n, d = t.shape
        return t.reshape(b, h, n // CHUNK, CHUNK, d).transpose(2, 0, 1, 3, 4)

    causal = jnp.tril(jnp.ones((CHUNK, CHUNK), dtype=bool))

    def step(state, xs):
        if q is None:
            kc, vc, gc = xs
        else:
            qc, kc, vc, gc = xs
        b = jnp.cumsum(gc, axis=2)
        b_last = b[:, :, -1:, :]
        k_dec = kc * jnp.exp(b_last - b)
        new_state = (jnp.exp(b_last[:, :, 0, :, None]) * state
                     + jnp.einsum('bhck,bhcv->bhkv', k_dec, vc))
        if q is None:
            return new_state, None
        o_inter = jnp.einsum('bhtk,bhkv->bhtv', qc * jnp.exp(b), state)
        diff = b[:, :, :, None, :] - b[:, :, None, :, :]
        decay = jnp.exp(jnp.where(causal[:, :, None], diff, -jnp.inf))
        scores = jnp.einsum('bhtk,bhsk,bhtsk->bhts', qc, kc, decay)
        o = o_inter + jnp.einsum('bhts,bhsv->bhtv', scores, vc)
        return new_state, o

    inputs = (k, v, log_f) if q is None else (q, k, v, log_f)
    xs = tuple(to_chunks(t) for t in inputs)
    state, o = lax.scan(step, s0, xs)
    if q is None:
        return state, None
    nc, b, h, cl, dv = o.shape
    return state, o.transpose(1, 2, 0, 3, 4).reshape(b, h, nc * cl, dv)


def hgrn_bidir(p_i, p_ff, p_fb, lb_l, s0_f, s0_b, q=None):
    v = split_heads(p_i.astype(jnp.float32))
    logf_f, k_f = forget_gate(p_ff, lb_l[0])
    logf_b, k_b = forget_gate(p_fb, lb_l[1])
    rev = lambda t: t[:, :, ::-1]
    qh = None if q is None else split_heads(q.astype(jnp.float32))
    s_f, o_f = gla_scan(k_f, v, logf_f, s0_f, qh)
    s_b, o_b = gla_scan(rev(k_b), rev(v), rev(logf_b), s0_b, None if qh is None else rev(qh))
    if q is None:
        return None, s_f, s_b
    return o_f + rev(o_b), s_f, s_b


def hgrn_readout(o, g, norm_g):
    y = o * lax.rsqrt(jnp.mean(o * o, axis=-1, keepdims=True) + EPS)
    return merge_heads(y) * norm_g.astype(jnp.float32) * jax.nn.silu(g.astype(jnp.float32))


def token_mixer(h, p, lb_l, conv_fn, s0_f, s0_b):
    proj = jnp.einsum('bnd,dc->bnc', h, p['w_in'])
    conv_out = conv_module(proj, p['taps'], p['conv_bias'], p['ln_g'], p['ln_b'], conv_fn)
    o, s_f, s_b = hgrn_bidir(proj[..., OFF_I:OFF_FF], proj[..., OFF_FF:OFF_FB],
                             proj[..., OFF_FB:OFF_G], lb_l, s0_f, s0_b,
                             q=proj[..., OFF_Q:OFF_I])
    hg = hgrn_readout(o, proj[..., OFF_G:IN_COLS], p['hgrn_g']).astype(h.dtype)
    mixed = jnp.concatenate([conv_out.astype(h.dtype), hg], axis=-1)
    return jnp.einsum('bnc,cd->bnd', mixed, p['w_out']), s_f, s_b


def expert_choice_ffn(h, p):
    b, n, d = h.shape
    cap = CAPACITY_FACTOR * n // N_EXPERTS
    logits = jnp.einsum('bnd,de->bne', h, p['router']).astype(jnp.float32)
    probs = jax.nn.softmax(logits, axis=-1)
    gate, idx = lax.top_k(probs.transpose(0, 2, 1), cap)
    xs = jax.vmap(lambda hb, ib: hb[ib])(h, idx)
    hid = (jax.nn.silu(jnp.einsum('becd,edf->becf', xs, p['w_gate']))
           * jnp.einsum('becd,edf->becf', xs, p['w_up']))
    out = jnp.einsum('becf,efd->becd', hid, p['w_down']) * gate[..., None].astype(h.dtype)
    y = jax.vmap(lambda ob, ib: jnp.zeros((n, d), ob.dtype).at[ib.reshape(-1)].add(
        ob.reshape(-1, d)))(out, idx)
    return y.astype(h.dtype)


def setup_inputs(seed: int = 0) -> dict:
    key = jax.random.key(seed)
    ks = jax.random.split(key, 21)
    f32 = jnp.float32

    def nrm(k, shape, scale):
        return jax.random.normal(k, shape, f32) * scale

    return {
        'x': nrm(ks[0], (BATCH, SEQ, D_MODEL), 1.0),
        'c': nrm(ks[1], (BATCH, D_MODEL), 1.0),
        'ctx': nrm(ks[2], (BATCH, CTX_LEN, D_MODEL), 1.0),
        'c_ctx': nrm(ks[3], (D_MODEL,), 1.0),
        'ada_w': nrm(ks[4], (DEPTH, D_MODEL, N_MOD * D_MODEL), D_MODEL ** -0.5),
        'ada_b': nrm(ks[5], (DEPTH, N_MOD * D_MODEL), 0.02),
        'norm1_g': 1.0 + nrm(ks[6], (DEPTH, D_MODEL), 0.02),
        'w_in': nrm(ks[7], (DEPTH, D_MODEL, IN_COLS), D_MODEL ** -0.5),
        'conv_taps': nrm(ks[8], (DEPTH, CONV_K, CONV_WIDTH), CONV_K ** -0.5),
        'conv_bias': nrm(ks[9], (DEPTH, CONV_WIDTH), 0.02),
        'conv_ln_g': 1.0 + nrm(ks[10], (DEPTH, CONV_WIDTH), 0.02),
        'conv_ln_b': nrm(ks[11], (DEPTH, CONV_WIDTH), 0.02),
        'hgrn_lb_logits': nrm(ks[12], (DEPTH + 1, 2, HGRN_WIDTH), 0.5),
        'hgrn_norm_g': 1.0 + nrm(ks[13], (DEPTH, HGRN_WIDTH), 0.02),
        'w_out': nrm(ks[14], (DEPTH, MIX_WIDTH, D_MODEL), MIX_WIDTH ** -0.5),
        'norm2_g': 1.0 + nrm(ks[15], (DEPTH, D_MODEL), 0.02),
        'router_w': nrm(ks[16], (DEPTH, D_MODEL, N_EXPERTS), D_MODEL ** -0.5),
        'w_gate': nrm(ks[17], (DEPTH, N_EXPERTS, D_MODEL, EXPERT_FF), D_MODEL ** -0.5),
        'w_up': nrm(ks[18], (DEPTH, N_EXPERTS, D_MODEL, EXPERT_FF), D_MODEL ** -0.5),
        'w_down': nrm(ks[19], (DEPTH, N_EXPERTS, EXPERT_FF, D_MODEL), EXPERT_FF ** -0.5),
        'final_g': 1.0 + nrm(ks[20], (D_MODEL,), 0.02),
    }


def reference(x, c, ctx, c_ctx, ada_w, ada_b, norm1_g, w_in, conv_taps, conv_bias,
              conv_ln_g, conv_ln_b, hgrn_lb_logits, hgrn_norm_g, w_out, norm2_g,
              router_w, w_gate, w_up, w_down, final_g):
    lower_bounds = jnp.cumsum(jax.nn.softmax(hgrn_lb_logits.astype(jnp.float32), axis=0),
                              axis=0)[:DEPTH]
    bsz = x.shape[0]
    zero_state = jnp.zeros((bsz, HGRN_HEADS, HGRN_HEAD_DIM, HGRN_HEAD_DIM), jnp.float32)
    for l in range(DEPTH):
        p = {'w_in': w_in[l], 'taps': conv_taps[l], 'conv_bias': conv_bias[l],
             'ln_g': conv_ln_g[l], 'ln_b': conv_ln_b[l], 'hgrn_g': hgrn_norm_g[l],
             'w_out': w_out[l], 'router': router_w[l], 'w_gate': w_gate[l],
             'w_up': w_up[l], 'w_down': w_down[l]}
        sh1, sc1, g1, sh2, sc2, g2 = modulation(c, ada_w[l], ada_b[l])
        csh1, csc1, cg1, csh2, csc2, cg2 = modulation(c_ctx[None, :], ada_w[l], ada_b[l])

        h_ctx = modulate(rmsnorm(ctx, norm1_g[l]), csh1, csc1)
        if l == DEPTH - 1:
            p_ctx = jnp.einsum('bnd,dc->bnc', h_ctx, w_in[l][:, OFF_I:OFF_G])
            _, s_f, s_b = hgrn_bidir(p_ctx[..., :HGRN_WIDTH],
                                     p_ctx[..., HGRN_WIDTH:2 * HGRN_WIDTH],
                                     p_ctx[..., 2 * HGRN_WIDTH:], lower_bounds[l],
                                     zero_state, zero_state)
        else:
            mix_ctx, s_f, s_b = token_mixer(h_ctx, p, lower_bounds[l], dwconv_seq,
                                            zero_state, zero_state)
            ctx = ctx + cg1 * mix_ctx
            ctx = ctx + cg2 * expert_choice_ffn(modulate(rmsnorm(ctx, norm2_g[l]), csh2, csc2), p)

        h = modulate(rmsnorm(x, norm1_g[l]), sh1, sc1)
        mix, _, _ = token_mixer(h, p, lower_bounds[l], dwconv_grid, s_f, s_b)
        x = x + g1 * mix
        x = x + g2 * expert_choice_ffn(modulate(rmsnorm(x, norm2_g[l]), sh2, sc2), p)
    return rmsnorm(x, final_g)
```

```python
import functools

import jax
import jax.numpy as jnp
import numpy as np
from jax import lax
from jax.experimental import pallas as pl
from jax.experimental.pallas import tpu as pltpu

F32 = jnp.float32
BF16 = jnp.bfloat16
HIGHEST = lax.Precision.HIGHEST

CONV_WIDTH = 512
HGRN_WIDTH = 512
HEAD_DIM = 128
N_HEADS = HGRN_WIDTH // HEAD_DIM
CONV_K = 31
CONV_PAD = (CONV_K - 1) // 2
GRID_W = 64
CHUNK = 64
N_EXPERTS = 16
CAPACITY_FACTOR = 2
N_MOD = 6
EPS = 1e-6
OFF_CONV_B = CONV_WIDTH
OFF_Q = 2 * CONV_WIDTH
OFF_I = OFF_Q + HGRN_WIDTH
OFF_FF = OFF_I + HGRN_WIDTH
OFF_FB = OFF_FF + HGRN_WIDTH
OFF_G = OFF_FB + HGRN_WIDTH
IN_COLS = OFF_G + HGRN_WIDTH

SUBLANES = 8
LANES = 128
VMEM_LIMIT = 56 * 1024 * 1024

N_LEVELS = 6
EXP_ROWS = (2 + N_LEVELS) * CHUNK


def _cparams(*sem):
    return pltpu.CompilerParams(dimension_semantics=sem, vmem_limit_bytes=VMEM_LIMIT)


def _sigmoid(x):
    return 1.0 / (1.0 + jnp.exp(-x))


def _silu(x):
    return x * _sigmoid(x)


def _rms(x, g):
    return x * lax.rsqrt(jnp.mean(x * x, axis=-1, keepdims=True) + EPS) * g


def _dot_nt(a, b):
    return lax.dot_general(a, b, (((1,), (1,)), ((), ())), preferred_element_type=F32)


def _dot_tn(a, b):
    return lax.dot_general(a, b, (((0,), (0,)), ((), ())), preferred_element_type=F32)


def _dot3(m_bf16, g):
    g0 = g.astype(BF16)
    r1 = g - g0.astype(F32)
    g1 = r1.astype(BF16)
    g2 = (r1 - g1.astype(F32)).astype(BF16)
    acc = jnp.dot(m_bf16, g0, preferred_element_type=F32)
    acc = acc + jnp.dot(m_bf16, g1, preferred_element_type=F32)
    return acc + jnp.dot(m_bf16, g2, preferred_element_type=F32)


def _lower_bound(lbl_ref, direction):
    l = lbl_ref[:, direction:direction + 1, :]
    m = jnp.max(l, axis=0)
    e = jnp.exp(l - m[None])
    return e[0] / jnp.sum(e, axis=0)


def _mod_kernel(cv_ref, w_ref, b_ref, o_ref):
    s = _silu(cv_ref[...])
    o_ref[...] = jnp.dot(s, w_ref[...], precision=HIGHEST,
                         preferred_element_type=F32) + b_ref[...]


def _modulation(cvec, w, b):
    rows, d = cvec.shape
    cols = w.shape[1]
    return pl.pallas_call(
        _mod_kernel,
        grid=(cols // d,),
        in_specs=[pl.BlockSpec((rows, d), lambda j: (0, 0)),
                  pl.BlockSpec((d, d), lambda j: (0, j)),
                  pl.BlockSpec((1, d), lambda j: (0, j))],
        out_specs=pl.BlockSpec((rows, d), lambda j: (0, j)),
        out_shape=jax.ShapeDtypeStruct((rows, cols), F32),
        compiler_params=_cparams("parallel"),
        name="mod",
    )(cvec, w, b.reshape(1, cols))


def _ctx_kernel(ctx_ref, mod_ref, g_ref, w_ref, lbl_ref, tri_ref, o_ref):
    x = ctx_ref[0]
    h = _rms(x, g_ref[...]) * (1.0 + mod_ref[0, 1:2, :]) + mod_ref[0, 0:1, :]
    p = jnp.dot(h.astype(BF16), w_ref[...], preferred_element_type=F32)
    v = p[:, :HGRN_WIDTH].astype(BF16)
    for direction in range(2):
        z = p[:, (1 + direction) * HGRN_WIDTH:(2 + direction) * HGRN_WIDTH]
        lb = _lower_bound(lbl_ref, direction)
        f = lb + (1.0 - lb) * _sigmoid(z)
        expo = _dot3(tri_ref[direction], jnp.log(f))
        kd = ((1.0 - f) * jnp.exp(expo)).astype(BF16)
        for hd in range(N_HEADS):
            sl = slice(hd * HEAD_DIM, (hd + 1) * HEAD_DIM)
            o_ref[0, direction, hd] = _dot_tn(v[:, sl], kd[:, sl])


def _ctx_states(ctx, mod3, norm_g, w_ctx, lb_logits):
    bsz, length, d = ctx.shape
    idx = np.arange(length)
    tri = np.stack([idx[None, :] > idx[:, None], idx[None, :] < idx[:, None]])
    tri = jnp.asarray(tri.astype(np.float32), dtype=BF16)
    ctx_row = mod3.shape[0] - 1
    return pl.pallas_call(
        _ctx_kernel,
        grid=(bsz,),
        in_specs=[pl.BlockSpec((1, length, d), lambda b: (b, 0, 0)),
                  pl.BlockSpec((1, N_MOD, d), lambda b: (ctx_row, 0, 0)),
                  pl.BlockSpec((1, d), lambda b: (0, 0)),
                  pl.BlockSpec(w_ctx.shape, lambda b: (0, 0)),
                  pl.BlockSpec(lb_logits.shape, lambda b: (0, 0, 0)),
                  pl.BlockSpec(tri.shape, lambda b: (0, 0, 0))],
        out_specs=pl.BlockSpec((1, 2, N_HEADS, HEAD_DIM, HEAD_DIM), lambda b: (b, 0, 0, 0, 0)),
        out_shape=jax.ShapeDtypeStruct((bsz, 2, N_HEADS, HEAD_DIM, HEAD_DIM), F32),
        compiler_params=_cparams("parallel"),
        name="ctx",
    )(ctx, mod3, norm_g, w_ctx, lb_logits, tri)


def _proj_kernel(x_ref, mod_ref, g_ref, w_ref, lbl_ref,
                 u_ref, q_ref, v_ref, lff_ref, lfb_ref, gs_ref):
    x = x_ref[0]
    h = _rms(x, g_ref[...]) * (1.0 + mod_ref[0, 1:2, :]) + mod_ref[0, 0:1, :]
    hb = h.astype(BF16)

    def cols(off, width):
        return jnp.dot(hb, w_ref[:, off:off + width], preferred_element_type=F32)

    u_ref[0] = (cols(0, CONV_WIDTH) * _sigmoid(cols(OFF_CONV_B, CONV_WIDTH))).astype(BF16)
    q_ref[0] = cols(OFF_Q, HGRN_WIDTH).astype(BF16)
    v_ref[0] = cols(OFF_I, HGRN_WIDTH).astype(BF16)
    for direction, (off, out) in enumerate(((OFF_FF, lff_ref), (OFF_FB, lfb_ref))):
        lb = _lower_bound(lbl_ref, direction)
        out[0] = jnp.log(lb + (1.0 - lb) * _sigmoid(cols(off, HGRN_WIDTH)))
    gs_ref[0] = _silu(cols(OFF_G, HGRN_WIDTH)).astype(BF16)


def _project(x, mod3, norm_g, w_in, lb_logits, tn):
    bsz, n, d = x.shape
    tok = lambda b, j: (b, j, 0)
    out_block = pl.BlockSpec((1, tn, HGRN_WIDTH), tok)
    sds = lambda dt: jax.ShapeDtypeStruct((bsz, n, HGRN_WIDTH), dt)
    return pl.pallas_call(
        _proj_kernel,
        grid=(bsz, n // tn),
        in_specs=[pl.BlockSpec((1, tn, d), tok),
                  pl.BlockSpec((1, N_MOD, d), lambda b, j: (b, 0, 0)),
                  pl.BlockSpec((1, d), lambda b, j: (0, 0)),
                  pl.BlockSpec(w_in.shape, lambda b, j: (0, 0)),
                  pl.BlockSpec(lb_logits.shape, lambda b, j: (0, 0, 0))],
        out_specs=[out_block] * 6,
        out_shape=[sds(BF16), sds(BF16), sds(BF16), sds(F32), sds(F32), sds(BF16)],
        compiler_params=_cparams("parallel", "parallel"),
        name="proj",
    )(x, mod3, norm_g, w_in, lb_logits)


W_OFF = 16
W_PADDED = GRID_W + 2 * W_OFF


def _conv_kernel(u_ref, taps_ref, bias_ref, lng_ref, lnb_ref, o_ref, hbuf, vbuf):
    rows = u_ref.shape[1]
    half = CONV_WIDTH // 2
    zeros_h = jnp.zeros((rows, W_OFF, half), F32)
    hbuf[:, 0:W_OFF, :] = zeros_h
    hbuf[:, W_OFF + GRID_W:W_PADDED, :] = zeros_h
    hbuf[:, W_OFF:W_OFF + GRID_W, :] = u_ref[0, :, :, 0:half].astype(F32)
    zeros_v = jnp.zeros((W_OFF, GRID_W, half), F32)
    vbuf[0:W_OFF] = zeros_v
    vbuf[W_OFF + rows:W_OFF + rows + W_OFF] = zeros_v
    vbuf[W_OFF:W_OFF + rows] = u_ref[0, :, :, half:CONV_WIDTH].astype(F32)

    def row_body(r, carry):
        acc_h = jnp.zeros((GRID_W, half), F32)
        acc_v = jnp.zeros((GRID_W, half), F32)
        for k in range(CONV_K):
            shift = W_OFF + k - CONV_PAD
            acc_h = acc_h + hbuf[r, shift:shift + GRID_W, :] * taps_ref[k:k + 1, 0:half]
            acc_v = acc_v + vbuf[r + shift] * taps_ref[k:k + 1, half:CONV_WIDTH]
        acc_h = acc_h + bias_ref[:, 0:half]
        acc_v = acc_v + bias_ref[:, half:CONV_WIDTH]
        mu = (jnp.sum(acc_h, axis=-1, keepdims=True)
              + jnp.sum(acc_v, axis=-1, keepdims=True)) / CONV_WIDTH
        ch = acc_h - mu
        cv = acc_v - mu
        var = (jnp.sum(ch * ch, axis=-1, keepdims=True)
               + jnp.sum(cv * cv, axis=-1, keepdims=True)) / CONV_WIDTH
        rs = lax.rsqrt(var + EPS)
        yh = ch * rs * lng_ref[:, 0:half] + lnb_ref[:, 0:half]
        yv = cv * rs * lng_ref[:, half:CONV_WIDTH] + lnb_ref[:, half:CONV_WIDTH]
        o_ref[0, r, :, 0:half] = _silu(yh).astype(BF16)
        o_ref[0, r, :, half:CONV_WIDTH] = _silu(yv).astype(BF16)
        return carry

    lax.fori_loop(0, rows, row_body, 0)


def _conv_module(u, taps, bias, ln_g, ln_b):
    bsz, n, cw = u.shape
    rows = n // GRID_W
    half = cw // 2
    u4 = u.reshape(bsz, rows, GRID_W, cw)
    vec = lambda a: a.reshape(1, cw)
    out = pl.pallas_call(
        _conv_kernel,
        grid=(bsz,),
        in_specs=[pl.BlockSpec((1, rows, GRID_W, cw), lambda b: (b, 0, 0, 0)),
                  pl.BlockSpec((CONV_K, cw), lambda b: (0, 0)),
                  pl.BlockSpec((1, cw), lambda b: (0, 0)),
                  pl.BlockSpec((1, cw), lambda b: (0, 0)),
                  pl.BlockSpec((1, cw), lambda b: (0, 0))],
        out_specs=pl.BlockSpec((1, rows, GRID_W, cw), lambda b: (b, 0, 0, 0)),
        out_shape=jax.ShapeDtypeStruct((bsz, rows, GRID_W, cw), BF16),
        scratch_shapes=[pltpu.VMEM((rows, W_PADDED, half), F32),
                        pltpu.VMEM((rows + 2 * W_OFF, GRID_W, half), F32)],
        compiler_params=_cparams("parallel"),
        name="conv",
    )(u4, taps, vec(bias), vec(ln_g), vec(ln_b))
    return out.reshape(bsz, n, cw)


def _scan_constants():
    t = np.arange(CHUNK)
    mats = [(t[None, :] <= t[:, None]), (t[None, :] > t[:, None])]
    masks = []
    for level in range(N_LEVELS):
        s = CHUNK >> (level + 1)
        mid = (t // (2 * s)) * (2 * s) + s
        upper = t >= mid
        m_up = (t[None, :] >= mid[:, None]) & (t[None, :] <= t[:, None])
        m_lo = (t[None, :] > t[:, None]) & (t[None, :] < mid[:, None])
        mats.append(np.where(upper[:, None], m_up, m_lo))
        same = (t[:, None] // (2 * s)) == (t[None, :] // (2 * s))
        masks.append(same & upper[:, None] & ~upper[None, :])
    masks.append(t[:, None] == t[None, :])
    m_f = np.concatenate(mats, axis=0).astype(np.float32)
    k_f = np.stack(masks).astype(np.float32)
    m_b = m_f.reshape(-1, CHUNK, CHUNK)[:, ::-1, ::-1].reshape(-1, CHUNK)
    k_b = k_f[:, ::-1, ::-1]
    return (jnp.asarray(np.stack([m_f, m_b]), dtype=BF16),
            jnp.asarray(np.stack([k_f, k_b]), dtype=F32))


def _scan_kernel(q_ref, v_ref, lff_ref, lfb_ref, gs_ref, s0_ref, ng_ref, m_ref, pm_ref,
                 o_ref, acc_ref, st_ref):
    n = q_ref.shape[1]
    n_chunks = n // CHUNK
    acc_ref[...] = jnp.zeros_like(acc_ref)
    st_ref[...] = s0_ref[0, :, 0]

    def chunk_step(direction, lf_ref, row0, total_row):
        rows = pl.ds(row0, CHUNK)
        g = lf_ref[0, rows, :]
        q = q_ref[0, rows, :].astype(F32)
        v = v_ref[0, rows, :]
        e_all = jnp.exp(_dot3(m_ref[direction], g))
        k = 1.0 - jnp.exp(g)
        st = st_ref[direction]
        out = _dot_nt((q * e_all[0:CHUNK]).astype(BF16), st.astype(BF16))
        scores = jnp.zeros((CHUNK, CHUNK), F32)
        for level in range(N_LEVELS):
            e_l = e_all[(2 + level) * CHUNK:(3 + level) * CHUNK]
            s_l = _dot_nt((q * e_l).astype(BF16), (k * e_l).astype(BF16))
            scores = jnp.where(pm_ref[direction, level] > 0.5, s_l, scores)
        s_d = _dot_nt(q.astype(BF16), k.astype(BF16))
        scores = jnp.where(pm_ref[direction, N_LEVELS] > 0.5, s_d, scores)
        out = out + jnp.dot(scores.astype(BF16), v, preferred_element_type=F32)
        kd = (k * e_all[CHUNK:2 * CHUNK]).astype(BF16)
        st_ref[direction] = st * e_all[total_row:total_row + 1] + _dot_tn(v, kd)
        acc_ref[rows, :] = acc_ref[rows, :] + out

    def body(c, carry):
        chunk_step(0, lff_ref, pl.multiple_of(c * CHUNK, CHUNK), CHUNK - 1)
        chunk_step(1, lfb_ref, pl.multiple_of((n_chunks - 1 - c) * CHUNK, CHUNK), 0)
        return carry

    lax.fori_loop(0, n_chunks, body, 0)

    tile = 256

    def readout(i, carry):
        rows = pl.ds(pl.multiple_of(i * tile, tile), tile)
        o = acc_ref[rows, :]
        y = o * lax.rsqrt(jnp.mean(o * o, axis=-1, keepdims=True) + EPS)
        o_ref[0, rows, :] = (y * ng_ref[...] * gs_ref[0, rows, :].astype(F32)).astype(BF16)
        return carry

    lax.fori_loop(0, n // tile, readout, 0)


def _hgrn_scan(q, v, lff, lfb, gs, states, norm_g):
    bsz, n, width = q.shape
    m_const, pm_const = _scan_constants()
    head = pl.BlockSpec((1, n, HEAD_DIM), lambda b, h: (b, 0, h))
    return pl.pallas_call(
        _scan_kernel,
        grid=(bsz, N_HEADS),
        in_specs=[head, head, head, head, head,
                  pl.BlockSpec((1, 2, 1, HEAD_DIM, HEAD_DIM), lambda b, h: (b, 0, h, 0, 0)),
                  pl.BlockSpec((1, HEAD_DIM), lambda b, h: (0, h)),
                  pl.BlockSpec(m_const.shape, lambda b, h: (0, 0, 0)),
                  pl.BlockSpec(pm_const.shape, lambda b, h: (0, 0, 0, 0))],
        out_specs=head,
        out_shape=jax.ShapeDtypeStruct((bsz, n, width), BF16),
        scratch_shapes=[pltpu.VMEM((n, HEAD_DIM), F32),
                        pltpu.VMEM((2, HEAD_DIM, HEAD_DIM), F32)],
        compiler_params=_cparams("parallel", "parallel"),
        name="scan",
    )(q, v, lff, lfb, gs, states, norm_g.reshape(1, width), m_const, pm_const)


def _oproj_kernel(cv_ref, hg_ref, x_ref, mod_ref, w_ref, g2_ref, rw_ref,
                  x1_ref, h2_ref, pr_ref):
    mix = jnp.dot(cv_ref[0], w_ref[0:CONV_WIDTH, :], preferred_element_type=F32)
    mix = mix + jnp.dot(hg_ref[0], w_ref[CONV_WIDTH:, :], preferred_element_type=F32)
    x1 = x_ref[0] + mod_ref[0, 2:3, :] * mix
    x1_ref[0] = x1
    h2 = _rms(x1, g2_ref[...]) * (1.0 + mod_ref[0, 4:5, :]) + mod_ref[0, 3:4, :]
    h2_ref[0] = h2.astype(BF16)
    logits = lax.dot_general(rw_ref[...], h2, (((1,), (1,)), ((), ())),
                             precision=HIGHEST, preferred_element_type=F32)
    z = jnp.exp(logits - jnp.max(logits, axis=0, keepdims=True))
    pr_ref[0] = z / jnp.sum(z, axis=0, keepdims=True)


def _out_project(conv_out, hg, x, mod3, w_out, norm2_g, router_t, tn):
    bsz, n, d = x.shape
    tok = lambda b, j: (b, j, 0)
    return pl.pallas_call(
        _oproj_kernel,
        grid=(bsz, n // tn),
        in_specs=[pl.BlockSpec((1, tn, CONV_WIDTH), tok),
                  pl.BlockSpec((1, tn, HGRN_WIDTH), tok),
                  pl.BlockSpec((1, tn, d), tok),
                  pl.BlockSpec((1, N_MOD, d), lambda b, j: (b, 0, 0)),
                  pl.BlockSpec(w_out.shape, lambda b, j: (0, 0)),
                  pl.BlockSpec((1, d), lambda b, j: (0, 0)),
                  pl.BlockSpec(router_t.shape, lambda b, j: (0, 0))],
        out_specs=[pl.BlockSpec((1, tn, d), tok),
                   pl.BlockSpec((1, tn, d), tok),
                   pl.BlockSpec((1, N_EXPERTS, tn), lambda b, j: (b, 0, j))],
        out_shape=[jax.ShapeDtypeStruct((bsz, n, d), F32),
                   jax.ShapeDtypeStruct((bsz, n, d), BF16),
                   jax.ShapeDtypeStruct((bsz, N_EXPERTS, n), F32)],
        compiler_params=_cparams("parallel", "parallel"),
        name="oproj",
    )(conv_out, hg, x, mod3, w_out, norm2_g, router_t)


def _prefix_count(flags, tri_ref):
    n = flags.shape[1]
    run = jnp.zeros((flags.shape[0], 1), F32)
    pieces = []
    for j in range(n // LANES):
        blk = flags[:, j * LANES:(j + 1) * LANES]
        inc = jnp.dot(blk.astype(BF16), tri_ref[...], preferred_element_type=F32)
        pieces.append(inc - blk + run)
        run = run + inc[:, LANES - 1:LANES]
    return jnp.concatenate(pieces, axis=1), run


def _topk_kernel(p_ref, tri_ref, slot_ref, *, cap):
    p = p_ref[0]
    n_exp = p.shape[0]
    capf = jnp.float32(cap)

    def count_ge(t):
        return jnp.sum(jnp.where(p >= t, 1.0, 0.0), axis=1, keepdims=True)

    def cond(state):
        return state[2] > 0

    def body(state):
        lo, hi, _ = state
        mid = 0.5 * (lo + hi)
        ge = count_ge(mid) >= capf
        lo_n = jnp.where(ge, mid, lo)
        hi_n = jnp.where(ge, hi, mid)
        nxt = 0.5 * (lo_n + hi_n)
        open_ = jnp.where((nxt > lo_n) & (nxt < hi_n), 1, 0)
        return lo_n, hi_n, jnp.max(open_)

    lo0 = jnp.zeros((n_exp, 1), p.dtype)
    hi0 = jnp.full((n_exp, 1), 2.0, p.dtype)
    thr, _, _ = lax.while_loop(cond, body, (lo0, hi0, jnp.int32(1)))
    gt = jnp.where(p > thr, 1.0, 0.0)
    eq = jnp.where(p == thr, 1.0, 0.0)
    need = capf - jnp.sum(gt, axis=1, keepdims=True)
    eq_rank, _ = _prefix_count(eq, tri_ref)
    sel = gt + eq * jnp.where(eq_rank < need, 1.0, 0.0)
    slot, _ = _prefix_count(sel, tri_ref)
    slot_ref[0] = jnp.where(sel > 0.5, slot, -1.0).astype(jnp.int32)


def _expert_choice(probs, cap):
    bsz, n_exp, n = probs.shape
    idx = np.arange(LANES)
    tri = jnp.asarray((idx[:, None] <= idx[None, :]).astype(np.float32), dtype=BF16)
    return pl.pallas_call(
        functools.partial(_topk_kernel, cap=cap),
        grid=(bsz,),
        in_specs=[pl.BlockSpec((1, n_exp, n), lambda b: (b, 0, 0)),
                  pl.BlockSpec((LANES, LANES), lambda b: (0, 0))],
        out_specs=pl.BlockSpec((1, n_exp, n), lambda b: (b, 0, 0)),
        out_shape=jax.ShapeDtypeStruct((bsz, n_exp, n), jnp.int32),
        compiler_params=_cparams("parallel"),
        name="topk",
    )(probs, tri)


def _one_hot_rows(slot_row, cap):
    ids = lax.broadcasted_iota(jnp.int32, (cap, slot_row.shape[1]), 0)
    return jnp.where(ids == slot_row, 1.0, 0.0).astype(BF16)


def _ffn_kernel(h_ref, slot_ref, wg_ref, wu_ref, wd_ref, o_ref, xs_ref, *, cap, tt):
    n = h_ref.shape[1]
    xs_ref[...] = jnp.zeros_like(xs_ref)

    def gather(j, carry):
        cols = pl.ds(pl.multiple_of(j * tt, tt), tt)
        onehot = _one_hot_rows(slot_ref[0, 0, :, cols], cap)
        xs_ref[...] += jnp.dot(onehot, h_ref[0, cols, :], preferred_element_type=F32)
        return carry

    lax.fori_loop(0, n // tt, gather, 0)
    xs = xs_ref[...].astype(BF16)
    gate_t = _dot_nt(wg_ref[0], xs)
    up_t = _dot_nt(wu_ref[0], xs)
    hid_t = (_silu(gate_t) * up_t).astype(BF16)
    o_ref[0, 0] = jnp.dot(wd_ref[0], hid_t, preferred_element_type=F32).astype(BF16)


def _expert_ffn(h2, slot4, wg_t, wu_t, wd_t, cap, tt):
    bsz, n, d = h2.shape
    n_exp, ff, _ = wg_t.shape
    return pl.pallas_call(
        functools.partial(_ffn_kernel, cap=cap, tt=tt),
        grid=(bsz, n_exp),
        in_specs=[pl.BlockSpec((1, n, d), lambda b, e: (b, 0, 0)),
                  pl.BlockSpec((1, 1, 1, n), lambda b, e: (b, e, 0, 0)),
                  pl.BlockSpec((1, ff, d), lambda b, e: (e, 0, 0)),
                  pl.BlockSpec((1, ff, d), lambda b, e: (e, 0, 0)),
                  pl.BlockSpec((1, d, ff), lambda b, e: (e, 0, 0))],
        out_specs=pl.BlockSpec((1, 1, d, cap), lambda b, e: (b, e, 0, 0)),
        out_shape=jax.ShapeDtypeStruct((bsz, n_exp, d, cap), BF16),
        scratch_shapes=[pltpu.VMEM((cap, d), F32)],
        compiler_params=_cparams("parallel", "arbitrary"),
        name="ffn",
    )(h2, slot4, wg_t, wu_t, wd_t)


def _comb_kernel(out_ref, slot_ref, p_ref, y_ref, *, cap, tt):
    n = y_ref.shape[2]
    e = pl.program_id(1)

    @pl.when(e == 0)
    def _():
        y_ref[...] = jnp.zeros_like(y_ref)

    def scatter(j, carry):
        cols = pl.ds(pl.multiple_of(j * tt, tt), tt)
        onehot = _one_hot_rows(slot_ref[0, 0, :, cols], cap)
        part = jnp.dot(out_ref[0, 0], onehot, preferred_element_type=F32)
        y_ref[0, :, cols] = y_ref[0, :, cols] + part * p_ref[0, 0, :, cols]
        return carry

    lax.fori_loop(0, n // tt, scatter, 0)


def _combine(out_t, slot4, probs4, n, tt):
    bsz, n_exp, d, cap = out_t.shape
    row = pl.BlockSpec((1, 1, 1, n), lambda b, e: (b, e, 0, 0))
    return pl.pallas_call(
        functools.partial(_comb_kernel, cap=cap, tt=tt),
        grid=(bsz, n_exp),
        in_specs=[pl.BlockSpec((1, 1, d, cap), lambda b, e: (b, e, 0, 0)), row, row],
        out_specs=pl.BlockSpec((1, d, n), lambda b, e: (b, 0, 0)),
        out_shape=jax.ShapeDtypeStruct((bsz, d, n), F32),
        compiler_params=_cparams("parallel", "arbitrary"),
        name="comb",
    )(out_t, slot4, probs4)


def _final_kernel(x1_ref, yt_ref, mod_ref, g_ref, o_ref):
    x2 = x1_ref[0] + mod_ref[0, 5:6, :] * yt_ref[0].T
    o_ref[0] = _rms(x2, g_ref[...])


def _final(x1, y_t, mod3, final_g, tn):
    bsz, n, d = x1.shape
    tok = lambda b, j: (b, j, 0)
    return pl.pallas_call(
        _final_kernel,
        grid=(bsz, n // tn),
        in_specs=[pl.BlockSpec((1, tn, d), tok),
                  pl.BlockSpec((1, d, tn), lambda b, j: (b, 0, j)),
                  pl.BlockSpec((1, N_MOD, d), lambda b, j: (b, 0, 0)),
                  pl.BlockSpec((1, d), lambda b, j: (0, 0))],
        out_specs=pl.BlockSpec((1, tn, d), tok),
        out_shape=jax.ShapeDtypeStruct((bsz, n, d), F32),
        compiler_params=_cparams("parallel", "parallel"),
        name="final",
    )(x1, y_t, mod3, final_g)


def kernel(x, c, ctx, c_ctx, ada_w, ada_b, norm1_g, w_in, conv_taps, conv_bias, conv_ln_g,
           conv_ln_b, hgrn_lb_logits, hgrn_norm_g, w_out, norm2_g, router_w, w_gate, w_up,
           w_down, final_g):
    bsz, n, d = x.shape
    assert ada_w.shape[0] == 1, "single-layer stack"
    assert n % GRID_W == 0 and n % CHUNK == 0 and ctx.shape[1] % SUBLANES == 0
    cap = CAPACITY_FACTOR * n // N_EXPERTS
    tn = min(512, n)

    rows = -(-(bsz + 1) // SUBLANES) * SUBLANES
    cvec = jnp.zeros((rows, d), F32).at[:bsz].set(c).at[rows - 1].set(c_ctx)
    mod3 = _modulation(cvec, ada_w[0], ada_b[0]).reshape(rows, N_MOD, d)

    w_in_b = w_in[0].astype(BF16)
    g1 = norm1_g[0].reshape(1, d)
    states = _ctx_states(ctx, mod3, g1, w_in_b[:, OFF_I:OFF_G], hgrn_lb_logits)

    u, q, v, lff, lfb, gs = _project(x, mod3, g1, w_in_b, hgrn_lb_logits, tn)
    conv_out = _conv_module(u, conv_taps[0], conv_bias[0], conv_ln_g[0], conv_ln_b[0])
    hg = _hgrn_scan(q, v, lff, lfb, gs, states, hgrn_norm_g[0])

    x1, h2, probs = _out_project(conv_out, hg, x, mod3, w_out[0].astype(BF16),
                                 norm2_g[0].reshape(1, d), router_w[0].T, tn)
    slot = _expert_choice(probs, cap)
    slot4 = slot.reshape(bsz, N_EXPERTS, 1, n)
    probs4 = probs.reshape(bsz, N_EXPERTS, 1, n)
    wg_t = jnp.swapaxes(w_gate[0], 1, 2).astype(BF16)
    wu_t = jnp.swapaxes(w_up[0], 1, 2).astype(BF16)
    wd_t = jnp.swapaxes(w_down[0], 1, 2).astype(BF16)
    out_t = _expert_ffn(h2, slot4, wg_t, wu_t, wd_t, cap, tn)
    y_t = _combine(out_t, slot4, probs4, n, tn)
    return _final(x1, y_t, mod3, final_g.reshape(1, d), tn)
```

```python
import functools

import jax
import jax.numpy as jnp
import numpy as np
from jax import lax
from jax.experimental import pallas as pl
from jax.experimental.pallas import tpu as pltpu

F32 = jnp.float32
BF16 = jnp.bfloat16
HIGHEST = lax.Precision.HIGHEST

CONV_WIDTH = 512
HGRN_WIDTH = 512
HEAD_DIM = 128
N_HEADS = HGRN_WIDTH // HEAD_DIM
CONV_K = 31
CONV_PAD = (CONV_K - 1) // 2
GRID_W = 64
CHUNK = 64
N_EXPERTS = 16
CAPACITY_FACTOR = 2
N_MOD = 6
EPS = 1e-6
OFF_CONV_B = CONV_WIDTH
OFF_Q = 2 * CONV_WIDTH
OFF_I = OFF_Q + HGRN_WIDTH
OFF_FF = OFF_I + HGRN_WIDTH
OFF_FB = OFF_FF + HGRN_WIDTH
OFF_G = OFF_FB + HGRN_WIDTH
IN_COLS = OFF_G + HGRN_WIDTH

SUBLANES = 8
LANES = 128
MXU_DIM = 256
VMEM_LIMIT = 56 * 1024 * 1024

N_LEVELS = 6
EXP_ROWS = (2 + N_LEVELS) * CHUNK


def _cparams(*sem):
    return pltpu.CompilerParams(dimension_semantics=sem, vmem_limit_bytes=VMEM_LIMIT)


def _sigmoid(x):
    return 1.0 / (1.0 + jnp.exp(-x))


def _silu(x):
    return x * _sigmoid(x)


def _rms(x, g):
    return x * lax.rsqrt(jnp.mean(x * x, axis=-1, keepdims=True) + EPS) * g


def _dot_nt(a, b):
    return lax.dot_general(a, b, (((1,), (1,)), ((), ())), preferred_element_type=F32)


def _dot_tn(a, b):
    return lax.dot_general(a, b, (((0,), (0,)), ((), ())), preferred_element_type=F32)


def _dot3(m_bf16, g):
    g0 = g.astype(BF16)
    r1 = g - g0.astype(F32)
    g1 = r1.astype(BF16)
    g2 = (r1 - g1.astype(F32)).astype(BF16)
    acc = jnp.dot(m_bf16, g0, preferred_element_type=F32)
    acc = acc + jnp.dot(m_bf16, g1, preferred_element_type=F32)
    return acc + jnp.dot(m_bf16, g2, preferred_element_type=F32)


def _lower_bound(lbl_ref, direction):
    l = lbl_ref[:, direction:direction + 1, :]
    m = jnp.max(l, axis=0)
    e = jnp.exp(l - m[None])
    return e[0] / jnp.sum(e, axis=0)


def _mod_kernel(cv_ref, w_ref, b_ref, o_ref):
    s = _silu(cv_ref[...])
    o_ref[...] = jnp.dot(s, w_ref[...], precision=HIGHEST,
                         preferred_element_type=F32) + b_ref[...]


def _modulation(cvec, w, b):
    rows, d = cvec.shape
    cols = w.shape[1]
    return pl.pallas_call(
        _mod_kernel,
        grid=(cols // d,),
        in_specs=[pl.BlockSpec((rows, d), lambda j: (0, 0)),
                  pl.BlockSpec((d, d), lambda j: (0, j)),
                  pl.BlockSpec((1, d), lambda j: (0, j))],
        out_specs=pl.BlockSpec((rows, d), lambda j: (0, j)),
        out_shape=jax.ShapeDtypeStruct((rows, cols), F32),
        compiler_params=_cparams("parallel"),
        name="mod",
    )(cvec, w, b.reshape(1, cols))


def _ctx_kernel(ctx_ref, mod_ref, g_ref, w_ref, lbl_ref, tri_ref, o_ref):
    x = ctx_ref[0]
    h = _rms(x, g_ref[...]) * (1.0 + mod_ref[0, 1:2, :]) + mod_ref[0, 0:1, :]
    p = jnp.dot(h.astype(BF16), w_ref[...], preferred_element_type=F32)
    v = p[:, :HGRN_WIDTH].astype(BF16)
    for direction in range(2):
        z = p[:, (1 + direction) * HGRN_WIDTH:(2 + direction) * HGRN_WIDTH]
        lb = _lower_bound(lbl_ref, direction)
        f = lb + (1.0 - lb) * _sigmoid(z)
        expo = _dot3(tri_ref[direction], jnp.log(f))
        kd = ((1.0 - f) * jnp.exp(expo)).astype(BF16)
        for hd in range(N_HEADS):
            sl = slice(hd * HEAD_DIM, (hd + 1) * HEAD_DIM)
            o_ref[0, direction, hd] = _dot_tn(v[:, sl], kd[:, sl])


def _ctx_states(ctx, mod3, norm_g, w_ctx, lb_logits):
    bsz, length, d = ctx.shape
    idx = np.arange(length)
    tri = np.stack([idx[None, :] > idx[:, None], idx[None, :] < idx[:, None]])
    tri = jnp.asarray(tri.astype(np.float32), dtype=BF16)
    ctx_row = mod3.shape[0] - 1
    return pl.pallas_call(
        _ctx_kernel,
        grid=(bsz,),
        in_specs=[pl.BlockSpec((1, length, d), lambda b: (b, 0, 0)),
                  pl.BlockSpec((1, N_MOD, d), lambda b: (ctx_row, 0, 0)),
                  pl.BlockSpec((1, d), lambda b: (0, 0)),
                  pl.BlockSpec(w_ctx.shape, lambda b: (0, 0)),
                  pl.BlockSpec(lb_logits.shape, lambda b: (0, 0, 0)),
                  pl.BlockSpec(tri.shape, lambda b: (0, 0, 0))],
        out_specs=pl.BlockSpec((1, 2, N_HEADS, HEAD_DIM, HEAD_DIM), lambda b: (b, 0, 0, 0, 0)),
        out_shape=jax.ShapeDtypeStruct((bsz, 2, N_HEADS, HEAD_DIM, HEAD_DIM), F32),
        compiler_params=_cparams("parallel"),
        name="ctx",
    )(ctx, mod3, norm_g, w_ctx, lb_logits, tri)


def _proj_kernel(x_ref, mod_ref, g_ref, w_ref, lbl_ref,
                 u_ref, q_ref, v_ref, lff_ref, lfb_ref, gs_ref):
    x = x_ref[0]
    h = _rms(x, g_ref[...]) * (1.0 + mod_ref[0, 1:2, :]) + mod_ref[0, 0:1, :]
    hb = h.astype(BF16)

    def cols(off, width):
        return jnp.dot(hb, w_ref[:, off:off + width], preferred_element_type=F32)

    u_ref[0] = (cols(0, CONV_WIDTH) * _sigmoid(cols(OFF_CONV_B, CONV_WIDTH))).astype(BF16)
    q_ref[0] = cols(OFF_Q, HGRN_WIDTH).astype(BF16)
    v_ref[0] = cols(OFF_I, HGRN_WIDTH).astype(BF16)
    for direction, (off, out) in enumerate(((OFF_FF, lff_ref), (OFF_FB, lfb_ref))):
        lb = _lower_bound(lbl_ref, direction)
        out[0] = jnp.log(lb + (1.0 - lb) * _sigmoid(cols(off, HGRN_WIDTH)))
    gs_ref[0] = _silu(cols(OFF_G, HGRN_WIDTH)).astype(BF16)


def _project(x, mod3, norm_g, w_in, lb_logits, tn):
    bsz, n, d = x.shape
    tok = lambda b, j: (b, j, 0)
    out_block = pl.BlockSpec((1, tn, HGRN_WIDTH), tok)
    sds = lambda dt: jax.ShapeDtypeStruct((bsz, n, HGRN_WIDTH), dt)
    return pl.pallas_call(
        _proj_kernel,
        grid=(bsz, n // tn),
        in_specs=[pl.BlockSpec((1, tn, d), tok),
                  pl.BlockSpec((1, N_MOD, d), lambda b, j: (b, 0, 0)),
                  pl.BlockSpec((1, d), lambda b, j: (0, 0)),
                  pl.BlockSpec(w_in.shape, lambda b, j: (0, 0)),
                  pl.BlockSpec(lb_logits.shape, lambda b, j: (0, 0, 0))],
        out_specs=[out_block] * 6,
        out_shape=[sds(BF16), sds(BF16), sds(BF16), sds(F32), sds(F32), sds(BF16)],
        compiler_params=_cparams("parallel", "parallel"),
        name="proj",
    )(x, mod3, norm_g, w_in, lb_logits)


W_OFF = 16
W_PADDED = GRID_W + 2 * W_OFF


def _conv_kernel(u_ref, taps_ref, bias_ref, lng_ref, lnb_ref, o_ref, hbuf, vbuf):
    rows = u_ref.shape[1]
    half = CONV_WIDTH // 2
    zeros_h = jnp.zeros((rows, W_OFF, half), F32)
    hbuf[:, 0:W_OFF, :] = zeros_h
    hbuf[:, W_OFF + GRID_W:W_PADDED, :] = zeros_h
    hbuf[:, W_OFF:W_OFF + GRID_W, :] = u_ref[0, :, :, 0:half].astype(F32)
    zeros_v = jnp.zeros((W_OFF, GRID_W, half), F32)
    vbuf[0:W_OFF] = zeros_v
    vbuf[W_OFF + rows:W_OFF + rows + W_OFF] = zeros_v
    vbuf[W_OFF:W_OFF + rows] = u_ref[0, :, :, half:CONV_WIDTH].astype(F32)

    def row_body(r, carry):
        acc_h = jnp.zeros((GRID_W, half), F32)
        acc_v = jnp.zeros((GRID_W, half), F32)
        for k in range(CONV_K):
            shift = W_OFF + k - CONV_PAD
            acc_h = acc_h + hbuf[r, shift:shift + GRID_W, :] * taps_ref[k:k + 1, 0:half]
            acc_v = acc_v + vbuf[r + shift] * taps_ref[k:k + 1, half:CONV_WIDTH]
        acc_h = acc_h + bias_ref[:, 0:half]
        acc_v = acc_v + bias_ref[:, half:CONV_WIDTH]
        mu = (jnp.sum(acc_h, axis=-1, keepdims=True)
              + jnp.sum(acc_v, axis=-1, keepdims=True)) / CONV_WIDTH
        ch = acc_h - mu
        cv = acc_v - mu
        var = (jnp.sum(ch * ch, axis=-1, keepdims=True)
               + jnp.sum(cv * cv, axis=-1, keepdims=True)) / CONV_WIDTH
        rs = lax.rsqrt(var + EPS)
        yh = ch * rs * lng_ref[:, 0:half] + lnb_ref[:, 0:half]
        yv = cv * rs * lng_ref[:, half:CONV_WIDTH] + lnb_ref[:, half:CONV_WIDTH]
        o_ref[0, r, :, 0:half] = _silu(yh).astype(BF16)
        o_ref[0, r, :, half:CONV_WIDTH] = _silu(yv).astype(BF16)
        return carry

    lax.fori_loop(0, rows, row_body, 0)


def _conv_module(u, taps, bias, ln_g, ln_b):
    bsz, n, cw = u.shape
    rows = n // GRID_W
    half = cw // 2
    u4 = u.reshape(bsz, rows, GRID_W, cw)
    vec = lambda a: a.reshape(1, cw)
    out = pl.pallas_call(
        _conv_kernel,
        grid=(bsz,),
        in_specs=[pl.BlockSpec((1, rows, GRID_W, cw), lambda b: (b, 0, 0, 0)),
                  pl.BlockSpec((CONV_K, cw), lambda b: (0, 0)),
                  pl.BlockSpec((1, cw), lambda b: (0, 0)),
                  pl.BlockSpec((1, cw), lambda b: (0, 0)),
                  pl.BlockSpec((1, cw), lambda b: (0, 0))],
        out_specs=pl.BlockSpec((1, rows, GRID_W, cw), lambda b: (b, 0, 0, 0)),
        out_shape=jax.ShapeDtypeStruct((bsz, rows, GRID_W, cw), BF16),
        scratch_shapes=[pltpu.VMEM((rows, W_PADDED, half), F32),
                        pltpu.VMEM((rows + 2 * W_OFF, GRID_W, half), F32)],
        compiler_params=_cparams("parallel"),
        name="conv",
    )(u4, taps, vec(bias), vec(ln_g), vec(ln_b))
    return out.reshape(bsz, n, cw)


def _scan_constants():
    t = np.arange(CHUNK)
    mats = [(t[None, :] <= t[:, None]), (t[None, :] > t[:, None])]
    masks = []
    for level in range(N_LEVELS):
        s = CHUNK >> (level + 1)
        mid = (t // (2 * s)) * (2 * s) + s
        upper = t >= mid
        m_up = (t[None, :] >= mid[:, None]) & (t[None, :] <= t[:, None])
        m_lo = (t[None, :] > t[:, None]) & (t[None, :] < mid[:, None])
        mats.append(np.where(upper[:, None], m_up, m_lo))
        same = (t[:, None] // (2 * s)) == (t[None, :] // (2 * s))
        masks.append(same & upper[:, None] & ~upper[None, :])
    masks.append(t[:, None] == t[None, :])
    m_f = np.concatenate(mats, axis=0).astype(np.float32)
    k_f = np.stack(masks).astype(np.float32)
    m_b = m_f.reshape(-1, CHUNK, CHUNK)[:, ::-1, ::-1].reshape(-1, CHUNK)
    k_b = k_f[:, ::-1, ::-1]
    m_f, m_b = (np.concatenate([m, m, m], axis=1) for m in (m_f, m_b))
    return (jnp.asarray(np.stack([m_f, m_b]), dtype=BF16),
            jnp.asarray(np.stack([k_f, k_b]), dtype=F32))


def _scan_kernel(q_ref, v_ref, lff_ref, lfb_ref, gs_ref, s0_ref, ng_ref, m_ref, pm_ref,
                 o_ref, acc_ref, qe_ref, inc_ref, dec_ref, st_ref, *, unroll):
    n = q_ref.shape[1]
    n_chunks = n // CHUNK
    lf_refs = (lff_ref, lfb_ref)
    total_rows = (CHUNK - 1, 0)

    def local(c, carry):
        rows = pl.ds(pl.multiple_of(c * CHUNK, CHUNK), CHUNK)
        q = q_ref[0, rows, :].astype(F32)
        v = v_ref[0, rows, :]
        out = jnp.zeros((CHUNK, HEAD_DIM), F32)
        for direction in range(2):
            g = lf_refs[direction][0, rows, :]
            g0 = g.astype(BF16)
            r1 = g - g0.astype(F32)
            g1 = r1.astype(BF16)
            g2 = (r1 - g1.astype(F32)).astype(BF16)
            expo = jnp.dot(m_ref[direction], jnp.concatenate([g0, g1, g2], axis=0),
                           preferred_element_type=F32)
            e_all = jnp.exp(expo)
            k = 1.0 - jnp.exp(g)
            qe_ref[direction, rows, :] = (q * e_all[0:CHUNK]).astype(BF16)
            scores = _dot_nt(q.astype(BF16), k.astype(BF16))
            scores = jnp.where(pm_ref[direction, N_LEVELS] > 0.5, scores, 0.0)
            for level in range(N_LEVELS):
                e_l = e_all[(2 + level) * CHUNK:(3 + level) * CHUNK]
                s_l = _dot_nt((q * e_l).astype(BF16), (k * e_l).astype(BF16))
                scores = jnp.where(pm_ref[direction, level] > 0.5, s_l, scores)
            out = out + jnp.dot(scores.astype(BF16), v, preferred_element_type=F32)
            kd = (k * e_all[CHUNK:2 * CHUNK]).astype(BF16)
            inc_ref[direction, c] = _dot_tn(v, kd)
            tr = total_rows[direction]
            dec_ref[direction, c] = jnp.broadcast_to(e_all[tr:tr + 1], (SUBLANES, HEAD_DIM))
        acc_ref[rows, :] = out
        return carry

    lax.fori_loop(0, n_chunks, local, 0, unroll=unroll)

    def recur(i, states):
        nxt = []
        for direction, c in ((0, i), (1, n_chunks - 1 - i)):
            st = states[direction]
            st_ref[direction, c] = st.astype(BF16)
            nxt.append(st * dec_ref[direction, c, 0:1, :] + inc_ref[direction, c])
        return tuple(nxt)

    lax.fori_loop(0, n_chunks, recur, (s0_ref[0, 0, 0], s0_ref[0, 1, 0]))

    def readout(c, carry):
        rows = pl.ds(pl.multiple_of(c * CHUNK, CHUNK), CHUNK)
        o = acc_ref[rows, :]
        for direction in range(2):
            o = o + _dot_nt(qe_ref[direction, rows, :], st_ref[direction, c])
        y = o * lax.rsqrt(jnp.mean(o * o, axis=-1, keepdims=True) + EPS)
        o_ref[0, rows, :] = (y * ng_ref[...] * gs_ref[0, rows, :].astype(F32)).astype(BF16)
        return carry

    lax.fori_loop(0, n_chunks, readout, 0, unroll=unroll)


def _hgrn_scan(q, v, lff, lfb, gs, states, norm_g):
    bsz, n, width = q.shape
    n_chunks = n // CHUNK
    m_const, pm_const = _scan_constants()
    head = pl.BlockSpec((1, n, HEAD_DIM), lambda b, h: (b, 0, h))
    return pl.pallas_call(
        functools.partial(_scan_kernel, unroll=2),
        grid=(bsz, N_HEADS),
        in_specs=[head, head, head, head, head,
                  pl.BlockSpec((1, 2, 1, HEAD_DIM, HEAD_DIM), lambda b, h: (b, 0, h, 0, 0)),
                  pl.BlockSpec((1, HEAD_DIM), lambda b, h: (0, h)),
                  pl.BlockSpec(m_const.shape, lambda b, h: (0, 0, 0)),
                  pl.BlockSpec(pm_const.shape, lambda b, h: (0, 0, 0, 0))],
        out_specs=head,
        out_shape=jax.ShapeDtypeStruct((bsz, n, width), BF16),
        scratch_shapes=[pltpu.VMEM((n, HEAD_DIM), F32),
                        pltpu.VMEM((2, n, HEAD_DIM), BF16),
                        pltpu.VMEM((2, n_chunks, HEAD_DIM, HEAD_DIM), F32),
                        pltpu.VMEM((2, n_chunks, SUBLANES, HEAD_DIM), F32),
                        pltpu.VMEM((2, n_chunks, HEAD_DIM, HEAD_DIM), BF16)],
        compiler_params=_cparams("parallel", "parallel"),
        name="scan",
    )(q, v, lff, lfb, gs, states, norm_g.reshape(1, width), m_const, pm_const)


def _oproj_kernel(cv_ref, hg_ref, x_ref, mod_ref, w_ref, g2_ref, rw_ref,
                  x1_ref, h2t_ref, pr_ref):
    mix = jnp.dot(cv_ref[0], w_ref[0:CONV_WIDTH, :], preferred_element_type=F32)
    mix = mix + jnp.dot(hg_ref[0], w_ref[CONV_WIDTH:, :], preferred_element_type=F32)
    x1 = x_ref[0] + mod_ref[0, 2:3, :] * mix
    x1_ref[0] = x1
    h2 = _rms(x1, g2_ref[...]) * (1.0 + mod_ref[0, 4:5, :]) + mod_ref[0, 3:4, :]
    h2t_ref[0] = h2.T.astype(BF16)
    logits = lax.dot_general(rw_ref[...], h2, (((1,), (1,)), ((), ())),
                             precision=HIGHEST, preferred_element_type=F32)
    z = jnp.exp(logits - jnp.max(logits, axis=0, keepdims=True))
    pr_ref[0] = z / jnp.sum(z, axis=0, keepdims=True)


def _out_project(conv_out, hg, x, mod3, w_out, norm2_g, router_t, tn):
    bsz, n, d = x.shape
    tok = lambda b, j: (b, j, 0)
    return pl.pallas_call(
        _oproj_kernel,
        grid=(bsz, n // tn),
        in_specs=[pl.BlockSpec((1, tn, CONV_WIDTH), tok),
                  pl.BlockSpec((1, tn, HGRN_WIDTH), tok),
                  pl.BlockSpec((1, tn, d), tok),
                  pl.BlockSpec((1, N_MOD, d), lambda b, j: (b, 0, 0)),
                  pl.BlockSpec(w_out.shape, lambda b, j: (0, 0)),
                  pl.BlockSpec((1, d), lambda b, j: (0, 0)),
                  pl.BlockSpec(router_t.shape, lambda b, j: (0, 0))],
        out_specs=[pl.BlockSpec((1, tn, d), tok),
                   pl.BlockSpec((1, d, tn), lambda b, j: (b, 0, j)),
                   pl.BlockSpec((1, N_EXPERTS, tn), lambda b, j: (b, 0, j))],
        out_shape=[jax.ShapeDtypeStruct((bsz, n, d), F32),
                   jax.ShapeDtypeStruct((bsz, d, n), BF16),
                   jax.ShapeDtypeStruct((bsz, N_EXPERTS, n), F32)],
        compiler_params=_cparams("parallel", "parallel"),
        name="oproj",
    )(conv_out, hg, x, mod3, w_out, norm2_g, router_t)


def _prefix_count(flags, tri_ref):
    n = flags.shape[1]
    run = jnp.zeros((flags.shape[0], 1), F32)
    pieces = []
    for j in range(n // LANES):
        blk = flags[:, j * LANES:(j + 1) * LANES]
        inc = jnp.dot(blk.astype(BF16), tri_ref[...], preferred_element_type=F32)
        pieces.append(inc - blk + run)
        run = run + inc[:, LANES - 1:LANES]
    return jnp.concatenate(pieces, axis=1)


def _topk_kernel(p_ref, tri_ref, tile_ref, slot_ref, off_ref, *, cap):
    p = p_ref[0]
    n_exp = p.shape[0]
    capf = jnp.float32(cap)

    def count_ge(t):
        return jnp.sum(jnp.where(p >= t, 1.0, 0.0), axis=1, keepdims=True)

    def cond(state):
        return state[2] > 0

    def body(state):
        lo, hi, _ = state
        mid = 0.5 * (lo + hi)
        ge = count_ge(mid) >= capf
        lo_n = jnp.where(ge, mid, lo)
        hi_n = jnp.where(ge, hi, mid)
        nxt = 0.5 * (lo_n + hi_n)
        open_ = jnp.where((nxt > lo_n) & (nxt < hi_n), 1, 0)
        return lo_n, hi_n, jnp.max(open_)

    lo0 = jnp.zeros((n_exp, 1), p.dtype)
    hi0 = jnp.full((n_exp, 1), 2.0, p.dtype)
    thr, _, _ = lax.while_loop(cond, body, (lo0, hi0, jnp.int32(1)))
    gt = jnp.where(p > thr, 1.0, 0.0)
    eq = jnp.where(p == thr, 1.0, 0.0)
    need = capf - jnp.sum(gt, axis=1, keepdims=True)
    sel = gt + eq * jnp.where(_prefix_count(eq, tri_ref) < need, 1.0, 0.0)
    slot = _prefix_count(sel, tri_ref)
    slot_ref[0] = jnp.where(sel > 0.5, slot, -1.0).astype(jnp.int32)
    off_ref[0] = jnp.dot(sel.astype(BF16), tile_ref[...],
                         preferred_element_type=F32).astype(jnp.int32)


def _expert_choice(probs, cap, tt):
    bsz, n_exp, n = probs.shape
    idx = np.arange(LANES)
    tri = jnp.asarray((idx[:, None] <= idx[None, :]).astype(np.float32), dtype=BF16)
    assert n // tt < LANES
    tile = jnp.asarray((np.arange(n)[:, None] < idx[None, :] * tt).astype(np.float32), dtype=BF16)
    return pl.pallas_call(
        functools.partial(_topk_kernel, cap=cap),
        grid=(bsz,),
        in_specs=[pl.BlockSpec((1, n_exp, n), lambda b: (b, 0, 0)),
                  pl.BlockSpec((LANES, LANES), lambda b: (0, 0)),
                  pl.BlockSpec((n, LANES), lambda b: (0, 0))],
        out_specs=[pl.BlockSpec((1, n_exp, n), lambda b: (b, 0, 0)),
                   pl.BlockSpec((1, n_exp, LANES), lambda b: (b, 0, 0))],
        out_shape=[jax.ShapeDtypeStruct((bsz, n_exp, n), jnp.int32),
                   jax.ShapeDtypeStruct((bsz, n_exp, LANES), jnp.int32)],
        compiler_params=_cparams("parallel"),
        name="topk",
    )(probs, tri, tile)


def _one_hot_rows(slot_row, first, rows):
    ids = lax.broadcasted_iota(jnp.int32, (rows, slot_row.shape[1]), 0) + first
    return jnp.where(ids == slot_row, 1.0, 0.0).astype(BF16)


def _band_loop(off_ref, n_tt, tt, ts, cap, visit):
    base = (pl.program_id(0) * pl.num_programs(1) + pl.program_id(1)) * (n_tt + 1)

    def tile(j, carry):
        lo = off_ref[base + j]
        hi = off_ref[base + j + 1]
        cols = pl.ds(pl.multiple_of(j * tt, tt), tt)
        for i in range(cap // ts):
            pl.when((hi > lo) & (hi > i * ts) & (lo < (i + 1) * ts))(
                functools.partial(visit, cols, i))
        return carry

    lax.fori_loop(0, n_tt, tile, 0)


def _ffn_kernel(off_ref, ht_ref, slot_ref, wg_ref, wu_ref, wd_ref, o_ref, xs_ref,
                *, cap, tt, ts):
    xs_ref[...] = jnp.zeros_like(xs_ref)

    def gather(cols, i):
        onehot = _one_hot_rows(slot_ref[0, 0, :, cols], i * ts, ts)
        xs_ref[:, i * ts:(i + 1) * ts] += _dot_nt(ht_ref[0, :, cols], onehot)

    _band_loop(off_ref, ht_ref.shape[2] // tt, tt, ts, cap, gather)
    xs = xs_ref[...].astype(BF16)
    gate_t = jnp.dot(wg_ref[0], xs, preferred_element_type=F32)
    up_t = jnp.dot(wu_ref[0], xs, preferred_element_type=F32)
    hid_t = (_silu(gate_t) * up_t).astype(BF16)
    o_ref[0, 0] = jnp.dot(wd_ref[0], hid_t, preferred_element_type=F32).astype(BF16)


def _expert_ffn(off, h2t, slot4, wg_t, wu_t, wd_t, cap, tt, ts):
    bsz, d, n = h2t.shape
    n_exp, ff, _ = wg_t.shape
    return pl.pallas_call(
        functools.partial(_ffn_kernel, cap=cap, tt=tt, ts=ts),
        grid_spec=pltpu.PrefetchScalarGridSpec(
            num_scalar_prefetch=1,
            grid=(bsz, n_exp),
            in_specs=[pl.BlockSpec((1, d, n), lambda b, e, off: (b, 0, 0)),
                      pl.BlockSpec((1, 1, 1, n), lambda b, e, off: (b, e, 0, 0)),
                      pl.BlockSpec((1, ff, d), lambda b, e, off: (e, 0, 0)),
                      pl.BlockSpec((1, ff, d), lambda b, e, off: (e, 0, 0)),
                      pl.BlockSpec((1, d, ff), lambda b, e, off: (e, 0, 0))],
            out_specs=pl.BlockSpec((1, 1, d, cap), lambda b, e, off: (b, e, 0, 0)),
            scratch_shapes=[pltpu.VMEM((d, cap), F32)]),
        out_shape=jax.ShapeDtypeStruct((bsz, n_exp, d, cap), BF16),
        compiler_params=_cparams("parallel", "arbitrary"),
        name="ffn",
    )(off, h2t, slot4, wg_t, wu_t, wd_t)


def _comb_kernel(off_ref, out_ref, slot_ref, p_ref, y_ref, *, cap, tt, ts):
    @pl.when(pl.program_id(1) == 0)
    def _():
        y_ref[...] = jnp.zeros_like(y_ref)

    def scatter(cols, i):
        onehot = _one_hot_rows(slot_ref[0, 0, :, cols], i * ts, ts)
        part = jnp.dot(out_ref[0, 0, :, i * ts:(i + 1) * ts], onehot,
                       preferred_element_type=F32)
        y_ref[0, :, cols] = y_ref[0, :, cols] + part * p_ref[0, 0, :, cols]

    _band_loop(off_ref, y_ref.shape[2] // tt, tt, ts, cap, scatter)


def _combine(off, out_t, slot4, probs4, n, tt, ts):
    bsz, n_exp, d, cap = out_t.shape
    row = pl.BlockSpec((1, 1, 1, n), lambda b, e, off: (b, e, 0, 0))
    return pl.pallas_call(
        functools.partial(_comb_kernel, cap=cap, tt=tt, ts=ts),
        grid_spec=pltpu.PrefetchScalarGridSpec(
            num_scalar_prefetch=1,
            grid=(bsz, n_exp),
            in_specs=[pl.BlockSpec((1, 1, d, cap), lambda b, e, off: (b, e, 0, 0)), row, row],
            out_specs=pl.BlockSpec((1, d, n), lambda b, e, off: (b, 0, 0))),
        out_shape=jax.ShapeDtypeStruct((bsz, d, n), F32),
        compiler_params=_cparams("parallel", "arbitrary"),
        name="comb",
    )(off, out_t, slot4, probs4)


def _final_kernel(x1_ref, yt_ref, mod_ref, g_ref, o_ref):
    x2 = x1_ref[0] + mod_ref[0, 5:6, :] * yt_ref[0].T
    o_ref[0] = _rms(x2, g_ref[...])


def _final(x1, y_t, mod3, final_g, tn):
    bsz, n, d = x1.shape
    tok = lambda b, j: (b, j, 0)
    return pl.pallas_call(
        _final_kernel,
        grid=(bsz, n // tn),
        in_specs=[pl.BlockSpec((1, tn, d), tok),
                  pl.BlockSpec((1, d, tn), lambda b, j: (b, 0, j)),
                  pl.BlockSpec((1, N_MOD, d), lambda b, j: (b, 0, 0)),
                  pl.BlockSpec((1, d), lambda b, j: (0, 0))],
        out_specs=pl.BlockSpec((1, tn, d), tok),
        out_shape=jax.ShapeDtypeStruct((bsz, n, d), F32),
        compiler_params=_cparams("parallel", "parallel"),
        name="final",
    )(x1, y_t, mod3, final_g)


def kernel(x, c, ctx, c_ctx, ada_w, ada_b, norm1_g, w_in, conv_taps, conv_bias, conv_ln_g,
           conv_ln_b, hgrn_lb_logits, hgrn_norm_g, w_out, norm2_g, router_w, w_gate, w_up,
           w_down, final_g):
    bsz, n, d = x.shape
    assert ada_w.shape[0] == 1, "single-layer stack"
    assert n % GRID_W == 0 and n % CHUNK == 0 and ctx.shape[1] % SUBLANES == 0
    cap = CAPACITY_FACTOR * n // N_EXPERTS
    tn = min(512, n)

    rows = -(-(bsz + 1) // SUBLANES) * SUBLANES
    cvec = jnp.zeros((rows, d), F32).at[:bsz].set(c).at[rows - 1].set(c_ctx)
    mod3 = _modulation(cvec, ada_w[0], ada_b[0]).reshape(rows, N_MOD, d)

    w_in_b = w_in[0].astype(BF16)
    g1 = norm1_g[0].reshape(1, d)
    states = _ctx_states(ctx, mod3, g1, w_in_b[:, OFF_I:OFF_G], hgrn_lb_logits)

    u, q, v, lff, lfb, gs = _project(x, mod3, g1, w_in_b, hgrn_lb_logits, tn)
    conv_out = _conv_module(u, conv_taps[0], conv_bias[0], conv_ln_g[0], conv_ln_b[0])
    hg = _hgrn_scan(q, v, lff, lfb, gs, states, hgrn_norm_g[0])

    x1, h2t, probs = _out_project(conv_out, hg, x, mod3, w_out[0].astype(BF16),
                                  norm2_g[0].reshape(1, d), router_w[0].T, tn)
    tt = min(MXU_DIM, n)
    ts = min(MXU_DIM, cap)
    slot, off = _expert_choice(probs, cap, tt)
    off = off[:, :, :n // tt + 1].reshape(-1)
    slot4 = slot.reshape(bsz, N_EXPERTS, 1, n)
    probs4 = probs.reshape(bsz, N_EXPERTS, 1, n)
    wg_t = jnp.swapaxes(w_gate[0], 1, 2).astype(BF16)
    wu_t = jnp.swapaxes(w_up[0], 1, 2).astype(BF16)
    wd_t = jnp.swapaxes(w_down[0], 1, 2).astype(BF16)
    out_t = _expert_ffn(off, h2t, slot4, wg_t, wu_t, wd_t, cap, tt, ts)
    y_t = _combine(off, out_t, slot4, probs4, n, tt, ts)
    return _final(x1, y_t, mod3, final_g.reshape(1, d), tn)
```

```python
import functools

import jax
import jax.numpy as jnp
import numpy as np
from jax import lax
from jax.experimental import pallas as pl
from jax.experimental.pallas import tpu as pltpu

F32 = jnp.float32
BF16 = jnp.bfloat16
HIGHEST = lax.Precision.HIGHEST

CONV_WIDTH = 512
HGRN_WIDTH = 512
HEAD_DIM = 128
N_HEADS = HGRN_WIDTH // HEAD_DIM
CONV_K = 31
CONV_PAD = (CONV_K - 1) // 2
GRID_W = 64
CHUNK = 64
N_EXPERTS = 16
CAPACITY_FACTOR = 2
N_MOD = 6
EPS = 1e-6
OFF_CONV_B = CONV_WIDTH
OFF_Q = 2 * CONV_WIDTH
OFF_I = OFF_Q + HGRN_WIDTH
OFF_FF = OFF_I + HGRN_WIDTH
OFF_FB = OFF_FF + HGRN_WIDTH
OFF_G = OFF_FB + HGRN_WIDTH
IN_COLS = OFF_G + HGRN_WIDTH

SUBLANES = 8
LANES = 128
MXU_DIM = 256
VMEM_LIMIT = 56 * 1024 * 1024

N_LEVELS = 6


def _cparams(*sem, flags=None):
    return pltpu.CompilerParams(dimension_semantics=sem, vmem_limit_bytes=VMEM_LIMIT,
                                flags=flags)


def _sigmoid(x):
    return 1.0 / (1.0 + jnp.exp(-x))


def _silu(x):
    return x * _sigmoid(x)


def _rms(x, g):
    return x * lax.rsqrt(jnp.mean(x * x, axis=-1, keepdims=True) + EPS) * g


def _dot_nt(a, b):
    return lax.dot_general(a, b, (((1,), (1,)), ((), ())), preferred_element_type=F32)


def _dot_tn(a, b):
    return lax.dot_general(a, b, (((0,), (0,)), ((), ())), preferred_element_type=F32)


def _dot3(m_bf16, g):
    g0 = g.astype(BF16)
    r1 = g - g0.astype(F32)
    g1 = r1.astype(BF16)
    g2 = (r1 - g1.astype(F32)).astype(BF16)
    acc = jnp.dot(m_bf16, g0, preferred_element_type=F32)
    acc = acc + jnp.dot(m_bf16, g1, preferred_element_type=F32)
    return acc + jnp.dot(m_bf16, g2, preferred_element_type=F32)


def _lower_bound(lbl_ref, direction):
    l = lbl_ref[:, direction:direction + 1, :]
    m = jnp.max(l, axis=0)
    e = jnp.exp(l - m[None])
    return e[0] / jnp.sum(e, axis=0)


def _mod_kernel(cv_ref, w_ref, b_ref, o_ref):
    s = _silu(cv_ref[...])
    o_ref[...] = jnp.dot(s, w_ref[...], precision=HIGHEST,
                         preferred_element_type=F32) + b_ref[...]


def _modulation(cvec, w, b):
    rows, d = cvec.shape
    cols = w.shape[1]
    return pl.pallas_call(
        _mod_kernel,
        grid=(cols // d,),
        in_specs=[pl.BlockSpec((rows, d), lambda j: (0, 0)),
                  pl.BlockSpec((d, d), lambda j: (0, j)),
                  pl.BlockSpec((1, d), lambda j: (0, j))],
        out_specs=pl.BlockSpec((rows, d), lambda j: (0, j)),
        out_shape=jax.ShapeDtypeStruct((rows, cols), F32),
        compiler_params=_cparams("parallel"),
        name="mod",
    )(cvec, w, b.reshape(1, cols))


def _ctx_kernel(ctx_ref, mod_ref, g_ref, w_ref, lbl_ref, tri_ref, o_ref):
    x = ctx_ref[0]
    h = _rms(x, g_ref[...]) * (1.0 + mod_ref[0, 1:2, :]) + mod_ref[0, 0:1, :]
    p = jnp.dot(h.astype(BF16), w_ref[...], preferred_element_type=F32)
    v = p[:, :HGRN_WIDTH].astype(BF16)
    for direction in range(2):
        z = p[:, (1 + direction) * HGRN_WIDTH:(2 + direction) * HGRN_WIDTH]
        lb = _lower_bound(lbl_ref, direction)
        f = lb + (1.0 - lb) * _sigmoid(z)
        expo = _dot3(tri_ref[direction], jnp.log(f))
        kd = ((1.0 - f) * jnp.exp(expo)).astype(BF16)
        for hd in range(N_HEADS):
            sl = slice(hd * HEAD_DIM, (hd + 1) * HEAD_DIM)
            o_ref[0, direction, hd] = _dot_tn(v[:, sl], kd[:, sl])


def _ctx_states(ctx, mod3, norm_g, w_ctx, lb_logits):
    bsz, length, d = ctx.shape
    idx = np.arange(length)
    tri = np.stack([idx[None, :] > idx[:, None], idx[None, :] < idx[:, None]])
    tri = jnp.asarray(tri.astype(np.float32), dtype=BF16)
    ctx_row = mod3.shape[0] - 1
    return pl.pallas_call(
        _ctx_kernel,
        grid=(bsz,),
        in_specs=[pl.BlockSpec((1, length, d), lambda b: (b, 0, 0)),
                  pl.BlockSpec((1, N_MOD, d), lambda b: (ctx_row, 0, 0)),
                  pl.BlockSpec((1, d), lambda b: (0, 0)),
                  pl.BlockSpec(w_ctx.shape, lambda b: (0, 0)),
                  pl.BlockSpec(lb_logits.shape, lambda b: (0, 0, 0)),
                  pl.BlockSpec(tri.shape, lambda b: (0, 0, 0))],
        out_specs=pl.BlockSpec((1, 2, N_HEADS, HEAD_DIM, HEAD_DIM), lambda b: (b, 0, 0, 0, 0)),
        out_shape=jax.ShapeDtypeStruct((bsz, 2, N_HEADS, HEAD_DIM, HEAD_DIM), F32),
        compiler_params=_cparams("parallel"),
        name="ctx",
    )(ctx, mod3, norm_g, w_ctx, lb_logits, tri)


def _proj_kernel(x_ref, mod_ref, g_ref, w_ref, lbl_ref,
                 u_ref, q_ref, v_ref, lff_ref, lfb_ref, gs_ref):
    x = x_ref[0]
    h = _rms(x, g_ref[...]) * (1.0 + mod_ref[0, 1:2, :]) + mod_ref[0, 0:1, :]
    hb = h.astype(BF16)

    def cols(off, width):
        return jnp.dot(hb, w_ref[:, off:off + width], preferred_element_type=F32)

    u_ref[0] = (cols(0, CONV_WIDTH) * _sigmoid(cols(OFF_CONV_B, CONV_WIDTH))).astype(BF16)
    for direction, (off, out) in enumerate(((OFF_FF, lff_ref), (OFF_FB, lfb_ref))):
        lb = _lower_bound(lbl_ref, direction)
        out[0] = jnp.log(lb + (1.0 - lb) * _sigmoid(cols(off, HGRN_WIDTH)))
    gs_ref[0] = _silu(cols(OFF_G, HGRN_WIDTH)).astype(BF16)
    q_ref[0] = cols(OFF_Q, HGRN_WIDTH).astype(BF16)
    v_ref[0] = cols(OFF_I, HGRN_WIDTH).astype(BF16)


def _project(x, mod3, norm_g, w_in, lb_logits, tn):
    bsz, n, d = x.shape
    tok = lambda b, j: (b, j, 0)
    out_block = pl.BlockSpec((1, tn, HGRN_WIDTH), tok)
    sds = lambda dt: jax.ShapeDtypeStruct((bsz, n, HGRN_WIDTH), dt)
    return pl.pallas_call(
        _proj_kernel,
        grid=(bsz, n // tn),
        in_specs=[pl.BlockSpec((1, tn, d), tok),
                  pl.BlockSpec((1, N_MOD, d), lambda b, j: (b, 0, 0)),
                  pl.BlockSpec((1, d), lambda b, j: (0, 0)),
                  pl.BlockSpec(w_in.shape, lambda b, j: (0, 0)),
                  pl.BlockSpec(lb_logits.shape, lambda b, j: (0, 0, 0))],
        out_specs=[out_block] * 6,
        out_shape=[sds(BF16), sds(BF16), sds(BF16), sds(F32), sds(F32), sds(BF16)],
        compiler_params=_cparams("parallel", "parallel"),
        name="proj",
    )(x, mod3, norm_g, w_in, lb_logits)


W_OFF = 16
W_PADDED = GRID_W + 2 * W_OFF


def _conv_kernel(u_ref, taps_ref, bias_ref, lng_ref, lnb_ref, o_ref, hbuf, vbuf):
    rows = u_ref.shape[1]
    half = CONV_WIDTH // 2
    zeros_h = jnp.zeros((rows, W_OFF, half), F32)
    hbuf[:, 0:W_OFF, :] = zeros_h
    hbuf[:, W_OFF + GRID_W:W_PADDED, :] = zeros_h
    hbuf[:, W_OFF:W_OFF + GRID_W, :] = u_ref[0, :, :, 0:half].astype(F32)
    zeros_v = jnp.zeros((W_OFF, GRID_W, half), F32)
    vbuf[0:W_OFF] = zeros_v
    vbuf[W_OFF + rows:W_OFF + rows + W_OFF] = zeros_v
    vbuf[W_OFF:W_OFF + rows] = u_ref[0, :, :, half:CONV_WIDTH].astype(F32)

    def row_body(r, carry):
        acc_h = jnp.zeros((GRID_W, half), F32)
        span = GRID_W + SUBLANES
        for b in range(SUBLANES):
            part = None
            for k in range(b - 1, CONV_K, SUBLANES):
                if k < 0:
                    continue
                start = k + 1 - b
                term = hbuf[r, start:start + span, :] * taps_ref[k:k + 1, 0:half]
                part = term if part is None else part + term
            acc_h = acc_h + part[b:b + GRID_W]
        acc_v = jnp.zeros((GRID_W, half), F32)
        for k in range(CONV_K):
            shift = W_OFF + k - CONV_PAD
            acc_v = acc_v + vbuf[r + shift] * taps_ref[k:k + 1, half:CONV_WIDTH]
        acc_h = acc_h + bias_ref[:, 0:half]
        acc_v = acc_v + bias_ref[:, half:CONV_WIDTH]
        mu = (jnp.sum(acc_h, axis=-1, keepdims=True)
              + jnp.sum(acc_v, axis=-1, keepdims=True)) / CONV_WIDTH
        ch = acc_h - mu
        cv = acc_v - mu
        var = (jnp.sum(ch * ch, axis=-1, keepdims=True)
               + jnp.sum(cv * cv, axis=-1, keepdims=True)) / CONV_WIDTH
        rs = lax.rsqrt(var + EPS)
        yh = ch * rs * lng_ref[:, 0:half] + lnb_ref[:, 0:half]
        yv = cv * rs * lng_ref[:, half:CONV_WIDTH] + lnb_ref[:, half:CONV_WIDTH]
        o_ref[0, r, :, 0:half] = _silu(yh).astype(BF16)
        o_ref[0, r, :, half:CONV_WIDTH] = _silu(yv).astype(BF16)
        return carry

    lax.fori_loop(0, rows, row_body, 0)


def _conv_module(u, taps, bias, ln_g, ln_b):
    bsz, n, cw = u.shape
    rows = n // GRID_W
    half = cw // 2
    u4 = u.reshape(bsz, rows, GRID_W, cw)
    vec = lambda a: a.reshape(1, cw)
    out = pl.pallas_call(
        _conv_kernel,
        grid=(bsz,),
        in_specs=[pl.BlockSpec((1, rows, GRID_W, cw), lambda b: (b, 0, 0, 0)),
                  pl.BlockSpec((CONV_K, cw), lambda b: (0, 0)),
                  pl.BlockSpec((1, cw), lambda b: (0, 0)),
                  pl.BlockSpec((1, cw), lambda b: (0, 0)),
                  pl.BlockSpec((1, cw), lambda b: (0, 0))],
        out_specs=pl.BlockSpec((1, rows, GRID_W, cw), lambda b: (b, 0, 0, 0)),
        out_shape=jax.ShapeDtypeStruct((bsz, rows, GRID_W, cw), BF16),
        scratch_shapes=[pltpu.VMEM((rows, W_PADDED, half), F32),
                        pltpu.VMEM((rows + 2 * W_OFF, GRID_W, half), F32)],
        compiler_params=_cparams("parallel"),
        name="conv",
    )(u4, taps, vec(bias), vec(ln_g), vec(ln_b))
    return out.reshape(bsz, n, cw)


def _scan_constants():
    t = np.arange(CHUNK)
    cum_f = (t[None, :] <= t[:, None]).astype(np.float32)
    signs, masks = [], []
    for level in range(N_LEVELS):
        s = CHUNK >> (level + 1)
        upper = (t // s) % 2 == 1
        same = (t[:, None] // (2 * s)) == (t[None, :] // (2 * s))
        masks.append(same & upper[:, None] & ~upper[None, :])
        signs.append(np.where(upper, 1.0, -1.0))
    masks.append(t[:, None] == t[None, :])
    sign_f = np.broadcast_to(np.stack(signs[:-1])[:, :, None], (N_LEVELS - 1, CHUNK, HEAD_DIM))
    pair_f = np.stack(masks).astype(np.float32)
    cum = np.stack([np.tile(m, (1, 3)) for m in (cum_f, cum_f[::-1, ::-1])])
    sign = np.stack([sign_f, sign_f[:, ::-1]])
    pair = np.stack([pair_f, pair_f[:, ::-1, ::-1]])
    return (jnp.asarray(cum, dtype=BF16), jnp.asarray(sign, dtype=F32),
            jnp.asarray(pair, dtype=F32))


def _level_factors(x, f, sign_ref, direction):
    dk = x.shape[1]
    factors = []
    for level in range(N_LEVELS - 1):
        s = CHUNK >> (level + 1)
        refs = [jnp.broadcast_to(x[r:r + 1, :], (2 * s, dk))
                for r in range(s - 1 + direction, CHUNK, 2 * s)]
        x_ref = refs[0] if len(refs) == 1 else jnp.concatenate(refs, axis=0)
        factors.append(jnp.exp((x - x_ref) * sign_ref[direction, level]))
    factors.append(f)
    return factors


def _scan_kernel(q_ref, v_ref, lff_ref, lfb_ref, gs_ref, s0_ref, ng_ref, m_ref, sg_ref, pm_ref,
                 o_ref, acc_ref, qe_ref, inc_ref, dec_ref, st_ref, *, unroll):
    n = q_ref.shape[1]
    n_chunks = n // CHUNK
    lf_refs = (lff_ref, lfb_ref)
    total_rows = (CHUNK - 1, 0)

    def local(i, carry):
        chunks = [i * unroll + u for u in range(unroll)]
        rows = [pl.ds(pl.multiple_of(c * CHUNK, CHUNK), CHUNK) for c in chunks]
        chains = [(u, d) for u in range(unroll) for d in range(2)]
        q = [q_ref[0, r, :].astype(F32) for r in rows]
        qb = [t.astype(BF16) for t in q]
        v = [v_ref[0, r, :] for r in rows]
        g, x = {}, {}
        for u, d in chains:
            g[u, d] = lf_refs[d][0, rows[u], :]
            g0 = g[u, d].astype(BF16)
            r1 = g[u, d] - g0.astype(F32)
            g1 = r1.astype(BF16)
            g2 = (r1 - g1.astype(F32)).astype(BF16)
            x[u, d] = jnp.dot(m_ref[d], jnp.concatenate([g0, g1, g2], axis=0),
                              preferred_element_type=F32)
        kb, ql, kl, kd = {}, {}, {}, {}
        for u, d in chains:
            xc = x[u, d]
            x_tot = xc[total_rows[d]:total_rows[d] + 1, :]
            f = jnp.exp(g[u, d])
            k = 1.0 - f
            kb[u, d] = k.astype(BF16)
            qe_ref[rows[u], d * HEAD_DIM:(d + 1) * HEAD_DIM] = (q[u] * jnp.exp(xc)).astype(BF16)
            factors = _level_factors(xc, f, sg_ref, d)
            ql[u, d] = [(q[u] * e).astype(BF16) for e in factors]
            kl[u, d] = [(k * e).astype(BF16) for e in factors[:-1]] + [kb[u, d]]
            kd[u, d] = (k * jnp.exp(x_tot - xc)).astype(BF16)
            dec_ref[d, chunks[u]] = jnp.broadcast_to(jnp.exp(x_tot), (SUBLANES, HEAD_DIM))
        scores = {}
        for u, d in chains:
            s_d = _dot_nt(qb[u], kb[u, d])
            scores[u, d] = jnp.where(pm_ref[d, N_LEVELS] > 0.5, s_d, 0.0)
        for level in range(N_LEVELS):
            for u, d in chains:
                s_l = _dot_nt(ql[u, d][level], kl[u, d][level])
                scores[u, d] = jnp.where(pm_ref[d, level] > 0.5, s_l, scores[u, d])
        for u in range(unroll):
            out = jnp.dot(scores[u, 0].astype(BF16), v[u], preferred_element_type=F32)
            out = out + jnp.dot(scores[u, 1].astype(BF16), v[u], preferred_element_type=F32)
            acc_ref[rows[u], :] = out
        for u, d in chains:
            inc_ref[d, chunks[u]] = _dot_tn(v[u], kd[u, d])
        return carry

    lax.fori_loop(0, n_chunks // unroll, local, 0)

    def recur(i, states):
        nxt = []
        for direction, c in ((0, i), (1, n_chunks - 1 - i)):
            st = states[direction]
            st_ref[c, :, direction * HEAD_DIM:(direction + 1) * HEAD_DIM] = st.astype(BF16)
            nxt.append(st * dec_ref[direction, c, 0:1, :] + inc_ref[direction, c])
        return tuple(nxt)

    lax.fori_loop(0, n_chunks, recur, (s0_ref[0, 0, 0], s0_ref[0, 1, 0]))

    wide = 2 * unroll

    def readout(i, carry):
        chunks = [i * wide + u for u in range(wide)]
        rows = [pl.ds(pl.multiple_of(c * CHUNK, CHUNK), CHUNK) for c in chunks]
        inter = [_dot_nt(qe_ref[rows[u], :], st_ref[chunks[u]]) for u in range(wide)]
        for u in range(wide):
            o = acc_ref[rows[u], :] + inter[u]
            y = o * lax.rsqrt(jnp.mean(o * o, axis=-1, keepdims=True) + EPS)
            o_ref[0, rows[u], :] = (y * ng_ref[...]
                                    * gs_ref[0, rows[u], :].astype(F32)).astype(BF16)
        return carry

    lax.fori_loop(0, n_chunks // wide, readout, 0)


def _hgrn_scan(q, v, lff, lfb, gs, states, norm_g):
    bsz, n, width = q.shape
    n_chunks = n // CHUNK
    m_const, sg_const, pm_const = _scan_constants()
    head = pl.BlockSpec((1, n, HEAD_DIM), lambda b, h: (b, 0, h))
    return pl.pallas_call(
        functools.partial(_scan_kernel, unroll=4),
        grid=(bsz, N_HEADS),
        in_specs=[head, head, head, head, head,
                  pl.BlockSpec((1, 2, 1, HEAD_DIM, HEAD_DIM), lambda b, h: (b, 0, h, 0, 0)),
                  pl.BlockSpec((1, HEAD_DIM), lambda b, h: (0, h)),
                  pl.BlockSpec(m_const.shape, lambda b, h: (0, 0, 0)),
                  pl.BlockSpec(sg_const.shape, lambda b, h: (0, 0, 0, 0)),
                  pl.BlockSpec(pm_const.shape, lambda b, h: (0, 0, 0, 0))],
        out_specs=head,
        out_shape=jax.ShapeDtypeStruct((bsz, n, width), BF16),
        scratch_shapes=[pltpu.VMEM((n, HEAD_DIM), F32),
                        pltpu.VMEM((n, 2 * HEAD_DIM), BF16),
                        pltpu.VMEM((2, n_chunks, HEAD_DIM, HEAD_DIM), F32),
                        pltpu.VMEM((2, n_chunks, SUBLANES, HEAD_DIM), F32),
                        pltpu.VMEM((n_chunks, HEAD_DIM, 2 * HEAD_DIM), BF16)],
        compiler_params=_cparams("parallel", "parallel"),
        name="scan",
    )(q, v, lff, lfb, gs, states, norm_g.reshape(1, width), m_const, sg_const, pm_const)


def _oproj_kernel(cv_ref, hg_ref, x_ref, mod_ref, w_ref, g2_ref, rw_ref,
                  x1_ref, h2t_ref, pr_ref):
    mix = jnp.dot(cv_ref[0], w_ref[0:CONV_WIDTH, :], preferred_element_type=F32)
    mix = mix + jnp.dot(hg_ref[0], w_ref[CONV_WIDTH:, :], preferred_element_type=F32)
    x1 = x_ref[0] + mod_ref[0, 2:3, :] * mix
    x1_ref[0] = x1
    h2 = _rms(x1, g2_ref[...]) * (1.0 + mod_ref[0, 4:5, :]) + mod_ref[0, 3:4, :]
    h2t_ref[0] = h2.T.astype(BF16)
    logits = lax.dot_general(rw_ref[...], h2, (((1,), (1,)), ((), ())),
                             precision=HIGHEST, preferred_element_type=F32)
    z = jnp.exp(logits - jnp.max(logits, axis=0, keepdims=True))
    pr_ref[0] = z / jnp.sum(z, axis=0, keepdims=True)


def _out_project(conv_out, hg, x, mod3, w_out, norm2_g, router_t, tn):
    bsz, n, d = x.shape
    tok = lambda b, j: (b, j, 0)
    return pl.pallas_call(
        _oproj_kernel,
        grid=(bsz, n // tn),
        in_specs=[pl.BlockSpec((1, tn, CONV_WIDTH), tok),
                  pl.BlockSpec((1, tn, HGRN_WIDTH), tok),
                  pl.BlockSpec((1, tn, d), tok),
                  pl.BlockSpec((1, N_MOD, d), lambda b, j: (b, 0, 0)),
                  pl.BlockSpec(w_out.shape, lambda b, j: (0, 0)),
                  pl.BlockSpec((1, d), lambda b, j: (0, 0)),
                  pl.BlockSpec(router_t.shape, lambda b, j: (0, 0))],
        out_specs=[pl.BlockSpec((1, tn, d), tok),
                   pl.BlockSpec((1, d, tn), lambda b, j: (b, 0, j)),
                   pl.BlockSpec((1, N_EXPERTS, tn), lambda b, j: (b, 0, j))],
        out_shape=[jax.ShapeDtypeStruct((bsz, n, d), F32),
                   jax.ShapeDtypeStruct((bsz, d, n), BF16),
                   jax.ShapeDtypeStruct((bsz, N_EXPERTS, n), F32)],
        compiler_params=_cparams("parallel", "parallel"),
        name="oproj",
    )(conv_out, hg, x, mod3, w_out, norm2_g, router_t)


def _prefix_count(flags, tri_ref):
    n = flags.shape[1]
    run = jnp.zeros((flags.shape[0], 1), F32)
    pieces = []
    for j in range(n // LANES):
        blk = flags[:, j * LANES:(j + 1) * LANES]
        inc = jnp.dot(blk.astype(BF16), tri_ref[...], preferred_element_type=F32)
        pieces.append(inc - blk + run)
        run = run + inc[:, LANES - 1:LANES]
    return jnp.concatenate(pieces, axis=1)


def _topk_kernel(p_ref, tri_ref, tile_ref, slot_ref, off_ref, *, cap):
    p = p_ref[0]
    n_exp = p.shape[0]
    capf = jnp.float32(cap)

    def count_ge(t):
        return jnp.sum(jnp.where(p >= t, 1.0, 0.0), axis=1, keepdims=True)

    def cond(state):
        return state[2] > 0

    def body(state):
        lo, hi, _ = state
        mid = 0.5 * (lo + hi)
        ge = count_ge(mid) >= capf
        lo_n = jnp.where(ge, mid, lo)
        hi_n = jnp.where(ge, hi, mid)
        nxt = 0.5 * (lo_n + hi_n)
        open_ = jnp.where((nxt > lo_n) & (nxt < hi_n), 1, 0)
        return lo_n, hi_n, jnp.max(open_)

    lo0 = jnp.zeros((n_exp, 1), p.dtype)
    hi0 = jnp.full((n_exp, 1), 2.0, p.dtype)
    thr, _, _ = lax.while_loop(cond, body, (lo0, hi0, jnp.int32(1)))
    gt = jnp.where(p > thr, 1.0, 0.0)
    eq = jnp.where(p == thr, 1.0, 0.0)
    need = capf - jnp.sum(gt, axis=1, keepdims=True)
    sel = gt + eq * jnp.where(_prefix_count(eq, tri_ref) < need, 1.0, 0.0)
    slot = _prefix_count(sel, tri_ref)
    slot_ref[0] = jnp.where(sel > 0.5, slot, -1.0).astype(jnp.int32)
    off_ref[0] = jnp.dot(sel.astype(BF16), tile_ref[...],
                         preferred_element_type=F32).astype(jnp.int32)


def _expert_choice(probs, cap, tt):
    bsz, n_exp, n = probs.shape
    idx = np.arange(LANES)
    tri = jnp.asarray((idx[:, None] <= idx[None, :]).astype(np.float32), dtype=BF16)
    assert n // tt < LANES
    tile = jnp.asarray((np.arange(n)[:, None] < idx[None, :] * tt).astype(np.float32), dtype=BF16)
    return pl.pallas_call(
        functools.partial(_topk_kernel, cap=cap),
        grid=(bsz,),
        in_specs=[pl.BlockSpec((1, n_exp, n), lambda b: (b, 0, 0)),
                  pl.BlockSpec((LANES, LANES), lambda b: (0, 0)),
                  pl.BlockSpec((n, LANES), lambda b: (0, 0))],
        out_specs=[pl.BlockSpec((1, n_exp, n), lambda b: (b, 0, 0)),
                   pl.BlockSpec((1, n_exp, LANES), lambda b: (b, 0, 0))],
        out_shape=[jax.ShapeDtypeStruct((bsz, n_exp, n), jnp.int32),
                   jax.ShapeDtypeStruct((bsz, n_exp, LANES), jnp.int32)],
        compiler_params=_cparams("parallel"),
        name="topk",
    )(probs, tri, tile)


def _one_hot_rows(slot_row, first, rows):
    ids = lax.broadcasted_iota(jnp.int32, (rows, slot_row.shape[1]), 0) + first
    return jnp.where(ids == slot_row, 1.0, 0.0).astype(BF16)


def _band_loop(off_ref, n_tt, tt, ts, cap, visit):
    base = (pl.program_id(0) * pl.num_programs(1) + pl.program_id(1)) * (n_tt + 1)

    def tile(j, carry):
        lo = off_ref[base + j]
        hi = off_ref[base + j + 1]
        cols = pl.ds(pl.multiple_of(j * tt, tt), tt)
        for i in range(cap // ts):
            pl.when((hi > lo) & (hi > i * ts) & (lo < (i + 1) * ts))(
                functools.partial(visit, cols, i))
        return carry

    lax.fori_loop(0, n_tt, tile, 0)


def _ffn_kernel(off_ref, ht_ref, slot_ref, wg_ref, wu_ref, wd_ref, o_ref, xs_ref,
                *, cap, tt, ts):
    xs_ref[...] = jnp.zeros_like(xs_ref)

    def gather(cols, i):
        onehot = _one_hot_rows(slot_ref[0, 0, :, cols], i * ts, ts)
        xs_ref[:, i * ts:(i + 1) * ts] += _dot_nt(ht_ref[0, :, cols], onehot)

    _band_loop(off_ref, ht_ref.shape[2] // tt, tt, ts, cap, gather)
    xs = xs_ref[...].astype(BF16)
    gate_t = jnp.dot(wg_ref[0], xs, preferred_element_type=F32)
    up_t = jnp.dot(wu_ref[0], xs, preferred_element_type=F32)
    hid_t = (_silu(gate_t) * up_t).astype(BF16)
    o_ref[0, 0] = jnp.dot(wd_ref[0], hid_t, preferred_element_type=F32).astype(BF16)


def _expert_ffn(off, h2t, slot4, wg_t, wu_t, wd_t, cap, tt, ts):
    bsz, d, n = h2t.shape
    n_exp, ff, _ = wg_t.shape
    return pl.pallas_call(
        functools.partial(_ffn_kernel, cap=cap, tt=tt, ts=ts),
        grid_spec=pltpu.PrefetchScalarGridSpec(
            num_scalar_prefetch=1,
            grid=(bsz, n_exp),
            in_specs=[pl.BlockSpec((1, d, n), lambda b, e, off: (b, 0, 0)),
                      pl.BlockSpec((1, 1, 1, n), lambda b, e, off: (b, e, 0, 0)),
                      pl.BlockSpec((1, ff, d), lambda b, e, off: (e, 0, 0)),
                      pl.BlockSpec((1, ff, d), lambda b, e, off: (e, 0, 0)),
                      pl.BlockSpec((1, d, ff), lambda b, e, off: (e, 0, 0))],
            out_specs=pl.BlockSpec((1, 1, d, cap), lambda b, e, off: (b, e, 0, 0)),
            scratch_shapes=[pltpu.VMEM((d, cap), F32)]),
        out_shape=jax.ShapeDtypeStruct((bsz, n_exp, d, cap), BF16),
        compiler_params=_cparams("parallel", "arbitrary"),
        name="ffn",
    )(off, h2t, slot4, wg_t, wu_t, wd_t)


def _comb_kernel(off_ref, out_ref, slot_ref, p_ref, y_ref, *, cap, tt, ts):
    @pl.when(pl.program_id(1) == 0)
    def _():
        y_ref[...] = jnp.zeros_like(y_ref)

    def scatter(cols, i):
        onehot = _one_hot_rows(slot_ref[0, 0, :, cols], i * ts, ts)
        part = jnp.dot(out_ref[0, 0, :, i * ts:(i + 1) * ts], onehot,
                       preferred_element_type=F32)
        y_ref[0, :, cols] = y_ref[0, :, cols] + part * p_ref[0, 0, :, cols]

    _band_loop(off_ref, y_ref.shape[2] // tt, tt, ts, cap, scatter)


def _combine(off, out_t, slot4, probs4, n, tt, ts):
    bsz, n_exp, d, cap = out_t.shape
    row = pl.BlockSpec((1, 1, 1, n), lambda b, e, off: (b, e, 0, 0))
    return pl.pallas_call(
        functools.partial(_comb_kernel, cap=cap, tt=tt, ts=ts),
        grid_spec=pltpu.PrefetchScalarGridSpec(
            num_scalar_prefetch=1,
            grid=(bsz, n_exp),
            in_specs=[pl.BlockSpec((1, 1, d, cap), lambda b, e, off: (b, e, 0, 0)), row, row],
            out_specs=pl.BlockSpec((1, d, n), lambda b, e, off: (b, 0, 0))),
        out_shape=jax.ShapeDtypeStruct((bsz, d, n), F32),
        compiler_params=_cparams("parallel", "arbitrary"),
        name="comb",
    )(off, out_t, slot4, probs4)


def _final_kernel(x1_ref, yt_ref, mod_ref, g_ref, o_ref):
    x2 = x1_ref[0] + mod_ref[0, 5:6, :] * yt_ref[0].T
    o_ref[0] = _rms(x2, g_ref[...])


def _final(x1, y_t, mod3, final_g, tn):
    bsz, n, d = x1.shape
    tok = lambda b, j: (b, j, 0)
    return pl.pallas_call(
        _final_kernel,
        grid=(bsz, n // tn),
        in_specs=[pl.BlockSpec((1, tn, d), tok),
                  pl.BlockSpec((1, d, tn), lambda b, j: (b, 0, j)),
                  pl.BlockSpec((1, N_MOD, d), lambda b, j: (b, 0, 0)),
                  pl.BlockSpec((1, d), lambda b, j: (0, 0))],
        out_specs=pl.BlockSpec((1, tn, d), tok),
        out_shape=jax.ShapeDtypeStruct((bsz, n, d), F32),
        compiler_params=_cparams("parallel", "parallel"),
        name="final",
    )(x1, y_t, mod3, final_g)


def kernel(x, c, ctx, c_ctx, ada_w, ada_b, norm1_g, w_in, conv_taps, conv_bias, conv_ln_g,
           conv_ln_b, hgrn_lb_logits, hgrn_norm_g, w_out, norm2_g, router_w, w_gate, w_up,
           w_down, final_g):
    bsz, n, d = x.shape
    assert ada_w.shape[0] == 1, "single-layer stack"
    assert n % GRID_W == 0 and n % CHUNK == 0 and ctx.shape[1] % SUBLANES == 0
    cap = CAPACITY_FACTOR * n // N_EXPERTS
    tn = min(512, n)

    rows = -(-(bsz + 1) // SUBLANES) * SUBLANES
    cvec = jnp.zeros((rows, d), F32).at[:bsz].set(c).at[rows - 1].set(c_ctx)
    mod3 = _modulation(cvec, ada_w[0], ada_b[0]).reshape(rows, N_MOD, d)

    w_in_b = w_in[0].astype(BF16)
    g1 = norm1_g[0].reshape(1, d)
    states = _ctx_states(ctx, mod3, g1, w_in_b[:, OFF_I:OFF_G], hgrn_lb_logits)

    u, q, v, lff, lfb, gs = _project(x, mod3, g1, w_in_b, hgrn_lb_logits, tn)
    conv_out = _conv_module(u, conv_taps[0], conv_bias[0], conv_ln_g[0], conv_ln_b[0])
    hg = _hgrn_scan(q, v, lff, lfb, gs, states, hgrn_norm_g[0])

    x1, h2t, probs = _out_project(conv_out, hg, x, mod3, w_out[0].astype(BF16),
                                  norm2_g[0].reshape(1, d), router_w[0].T, tn)
    tt = min(2 * MXU_DIM, n)
    ts = min(MXU_DIM, cap)
    slot, off = _expert_choice(probs, cap, tt)
    off = off[:, :, :n // tt + 1].reshape(-1)
    slot4 = slot.reshape(bsz, N_EXPERTS, 1, n)
    probs4 = probs.reshape(bsz, N_EXPERTS, 1, n)
    wg_t = jnp.swapaxes(w_gate[0], 1, 2).astype(BF16)
    wu_t = jnp.swapaxes(w_up[0], 1, 2).astype(BF16)
    wd_t = jnp.swapaxes(w_down[0], 1, 2).astype(BF16)
    out_t = _expert_ffn(off, h2t, slot4, wg_t, wu_t, wd_t, cap, tt, ts)
    y_t = _combine(off, out_t, slot4, probs4, n, tt, ts)
    return _final(x1, y_t, mod3, final_g.reshape(1, d), tn)
```

```python
import functools

import jax
import jax.numpy as jnp
import numpy as np
from jax import lax
from jax.experimental import pallas as pl
from jax.experimental.pallas import tpu as pltpu

F32 = jnp.float32
BF16 = jnp.bfloat16
HIGHEST = lax.Precision.HIGHEST

CONV_WIDTH = 512
HGRN_WIDTH = 512
HEAD_DIM = 128
N_HEADS = HGRN_WIDTH // HEAD_DIM
CONV_K = 31
CONV_PAD = (CONV_K - 1) // 2
GRID_W = 64
CHUNK = 64
N_EXPERTS = 16
CAPACITY_FACTOR = 2
N_MOD = 6
EPS = 1e-6
OFF_CONV_B = CONV_WIDTH
OFF_Q = 2 * CONV_WIDTH
OFF_I = OFF_Q + HGRN_WIDTH
OFF_FF = OFF_I + HGRN_WIDTH
OFF_FB = OFF_FF + HGRN_WIDTH
OFF_G = OFF_FB + HGRN_WIDTH
IN_COLS = OFF_G + HGRN_WIDTH

SUBLANES = 8
LANES = 128
MXU_DIM = 256
VMEM_LIMIT = 56 * 1024 * 1024

N_LEVELS = 6


def _cparams(*sem, flags=None):
    return pltpu.CompilerParams(dimension_semantics=sem, vmem_limit_bytes=VMEM_LIMIT,
                                flags=flags)


def _sigmoid(x):
    return 1.0 / (1.0 + jnp.exp(-x))


def _silu(x):
    return x * _sigmoid(x)


def _rms(x, g):
    return x * lax.rsqrt(jnp.mean(x * x, axis=-1, keepdims=True) + EPS) * g


def _dot_nt(a, b):
    return lax.dot_general(a, b, (((1,), (1,)), ((), ())), preferred_element_type=F32)


def _dot_tn(a, b):
    return lax.dot_general(a, b, (((0,), (0,)), ((), ())), preferred_element_type=F32)


def _dot3(m_bf16, g):
    g0 = g.astype(BF16)
    r1 = g - g0.astype(F32)
    g1 = r1.astype(BF16)
    g2 = (r1 - g1.astype(F32)).astype(BF16)
    acc = jnp.dot(m_bf16, g0, preferred_element_type=F32)
    acc = acc + jnp.dot(m_bf16, g1, preferred_element_type=F32)
    return acc + jnp.dot(m_bf16, g2, preferred_element_type=F32)


def _lower_bound(lbl_ref, direction):
    l = lbl_ref[:, direction:direction + 1, :]
    m = jnp.max(l, axis=0)
    e = jnp.exp(l - m[None])
    return e[0] / jnp.sum(e, axis=0)


def _mod_kernel(cv_ref, w_ref, b_ref, o_ref):
    s = _silu(cv_ref[...])
    o_ref[...] = jnp.dot(s, w_ref[...], precision=HIGHEST,
                         preferred_element_type=F32) + b_ref[...]


def _modulation(cvec, w, b):
    rows, d = cvec.shape
    cols = w.shape[1]
    return pl.pallas_call(
        _mod_kernel,
        grid=(cols // d,),
        in_specs=[pl.BlockSpec((rows, d), lambda j: (0, 0)),
                  pl.BlockSpec((d, d), lambda j: (0, j)),
                  pl.BlockSpec((1, d), lambda j: (0, j))],
        out_specs=pl.BlockSpec((rows, d), lambda j: (0, j)),
        out_shape=jax.ShapeDtypeStruct((rows, cols), F32),
        compiler_params=_cparams("parallel"),
        name="mod",
    )(cvec, w, b.reshape(1, cols))


def _ctx_kernel(ctx_ref, mod_ref, g_ref, w_ref, lbl_ref, tri_ref, o_ref):
    x = ctx_ref[0]
    h = _rms(x, g_ref[...]) * (1.0 + mod_ref[0, 1:2, :]) + mod_ref[0, 0:1, :]
    p = jnp.dot(h.astype(BF16), w_ref[...], preferred_element_type=F32)
    v = p[:, :HGRN_WIDTH].astype(BF16)
    for direction in range(2):
        z = p[:, (1 + direction) * HGRN_WIDTH:(2 + direction) * HGRN_WIDTH]
        lb = _lower_bound(lbl_ref, direction)
        f = lb + (1.0 - lb) * _sigmoid(z)
        expo = _dot3(tri_ref[direction], jnp.log(f))
        kd = ((1.0 - f) * jnp.exp(expo)).astype(BF16)
        for hd in range(N_HEADS):
            sl = slice(hd * HEAD_DIM, (hd + 1) * HEAD_DIM)
            o_ref[0, direction, hd] = _dot_tn(v[:, sl], kd[:, sl])


def _ctx_states(ctx, mod3, norm_g, w_ctx, lb_logits):
    bsz, length, d = ctx.shape
    idx = np.arange(length)
    tri = np.stack([idx[None, :] > idx[:, None], idx[None, :] < idx[:, None]])
    tri = jnp.asarray(tri.astype(np.float32), dtype=BF16)
    ctx_row = mod3.shape[0] - 1
    return pl.pallas_call(
        _ctx_kernel,
        grid=(bsz,),
        in_specs=[pl.BlockSpec((1, length, d), lambda b: (b, 0, 0)),
                  pl.BlockSpec((1, N_MOD, d), lambda b: (ctx_row, 0, 0)),
                  pl.BlockSpec((1, d), lambda b: (0, 0)),
                  pl.BlockSpec(w_ctx.shape, lambda b: (0, 0)),
                  pl.BlockSpec(lb_logits.shape, lambda b: (0, 0, 0)),
                  pl.BlockSpec(tri.shape, lambda b: (0, 0, 0))],
        out_specs=pl.BlockSpec((1, 2, N_HEADS, HEAD_DIM, HEAD_DIM), lambda b: (b, 0, 0, 0, 0)),
        out_shape=jax.ShapeDtypeStruct((bsz, 2, N_HEADS, HEAD_DIM, HEAD_DIM), F32),
        compiler_params=_cparams("parallel"),
        name="ctx",
    )(ctx, mod3, norm_g, w_ctx, lb_logits, tri)


def _proj_kernel(x_ref, mod_ref, g_ref, w_ref, lbl_ref,
                 u_ref, q_ref, v_ref, lff_ref, lfb_ref, gs_ref):
    x = x_ref[0]
    h = _rms(x, g_ref[...]) * (1.0 + mod_ref[0, 1:2, :]) + mod_ref[0, 0:1, :]
    hb = h.astype(BF16)

    def cols(off, width):
        return jnp.dot(hb, w_ref[:, off:off + width], preferred_element_type=F32)

    u_ref[0] = (cols(0, CONV_WIDTH) * _sigmoid(cols(OFF_CONV_B, CONV_WIDTH))).astype(BF16)
    for direction, (off, out) in enumerate(((OFF_FF, lff_ref), (OFF_FB, lfb_ref))):
        lb = _lower_bound(lbl_ref, direction)
        out[0] = jnp.log(lb + (1.0 - lb) * _sigmoid(cols(off, HGRN_WIDTH)))
    gs_ref[0] = _silu(cols(OFF_G, HGRN_WIDTH)).astype(BF16)
    q_ref[0] = cols(OFF_Q, HGRN_WIDTH).astype(BF16)
    v_ref[0] = cols(OFF_I, HGRN_WIDTH).astype(BF16)


def _project(x, mod3, norm_g, w_in, lb_logits, tn):
    bsz, n, d = x.shape
    tok = lambda b, j: (b, j, 0)
    out_block = pl.BlockSpec((1, tn, HGRN_WIDTH), tok)
    sds = lambda dt: jax.ShapeDtypeStruct((bsz, n, HGRN_WIDTH), dt)
    return pl.pallas_call(
        _proj_kernel,
        grid=(bsz, n // tn),
        in_specs=[pl.BlockSpec((1, tn, d), tok),
                  pl.BlockSpec((1, N_MOD, d), lambda b, j: (b, 0, 0)),
                  pl.BlockSpec((1, d), lambda b, j: (0, 0)),
                  pl.BlockSpec(w_in.shape, lambda b, j: (0, 0)),
                  pl.BlockSpec(lb_logits.shape, lambda b, j: (0, 0, 0))],
        out_specs=[out_block] * 6,
        out_shape=[sds(BF16), sds(BF16), sds(BF16), sds(F32), sds(F32), sds(BF16)],
        compiler_params=_cparams("parallel", "parallel"),
        name="proj",
    )(x, mod3, norm_g, w_in, lb_logits)


W_OFF = 16
W_PADDED = GRID_W + 2 * W_OFF


def _conv_kernel(u_ref, taps_ref, bias_ref, lng_ref, lnb_ref, o_ref, hbuf, vbuf):
    rows = u_ref.shape[1]
    half = CONV_WIDTH // 2
    zeros_h = jnp.zeros((rows, W_OFF, half), F32)
    hbuf[:, 0:W_OFF, :] = zeros_h
    hbuf[:, W_OFF + GRID_W:W_PADDED, :] = zeros_h
    hbuf[:, W_OFF:W_OFF + GRID_W, :] = u_ref[0, :, :, 0:half].astype(F32)
    zeros_v = jnp.zeros((W_OFF, GRID_W, half), F32)
    vbuf[0:W_OFF] = zeros_v
    vbuf[W_OFF + rows:W_OFF + rows + W_OFF] = zeros_v
    vbuf[W_OFF:W_OFF + rows] = u_ref[0, :, :, half:CONV_WIDTH].astype(F32)

    def row_body(r, carry):
        acc_h = jnp.zeros((GRID_W, half), F32)
        span = GRID_W + SUBLANES
        for b in range(SUBLANES):
            part = None
            for k in range(b - 1, CONV_K, SUBLANES):
                if k < 0:
                    continue
                start = k + 1 - b
                term = hbuf[r, start:start + span, :] * taps_ref[k:k + 1, 0:half]
                part = term if part is None else part + term
            acc_h = acc_h + part[b:b + GRID_W]
        acc_v = jnp.zeros((GRID_W, half), F32)
        for k in range(CONV_K):
            shift = W_OFF + k - CONV_PAD
            acc_v = acc_v + vbuf[r + shift] * taps_ref[k:k + 1, half:CONV_WIDTH]
        acc_h = acc_h + bias_ref[:, 0:half]
        acc_v = acc_v + bias_ref[:, half:CONV_WIDTH]
        mu = (jnp.sum(acc_h, axis=-1, keepdims=True)
              + jnp.sum(acc_v, axis=-1, keepdims=True)) / CONV_WIDTH
        ch = acc_h - mu
        cv = acc_v - mu
        var = (jnp.sum(ch * ch, axis=-1, keepdims=True)
               + jnp.sum(cv * cv, axis=-1, keepdims=True)) / CONV_WIDTH
        rs = lax.rsqrt(var + EPS)
        yh = ch * rs * lng_ref[:, 0:half] + lnb_ref[:, 0:half]
        yv = cv * rs * lng_ref[:, half:CONV_WIDTH] + lnb_ref[:, half:CONV_WIDTH]
        o_ref[0, r, :, 0:half] = _silu(yh).astype(BF16)
        o_ref[0, r, :, half:CONV_WIDTH] = _silu(yv).astype(BF16)
        return carry

    lax.fori_loop(0, rows, row_body, 0)


def _conv_module(u, taps, bias, ln_g, ln_b):
    bsz, n, cw = u.shape
    rows = n // GRID_W
    half = cw // 2
    u4 = u.reshape(bsz, rows, GRID_W, cw)
    vec = lambda a: a.reshape(1, cw)
    out = pl.pallas_call(
        _conv_kernel,
        grid=(bsz,),
        in_specs=[pl.BlockSpec((1, rows, GRID_W, cw), lambda b: (b, 0, 0, 0)),
                  pl.BlockSpec((CONV_K, cw), lambda b: (0, 0)),
                  pl.BlockSpec((1, cw), lambda b: (0, 0)),
                  pl.BlockSpec((1, cw), lambda b: (0, 0)),
                  pl.BlockSpec((1, cw), lambda b: (0, 0))],
        out_specs=pl.BlockSpec((1, rows, GRID_W, cw), lambda b: (b, 0, 0, 0)),
        out_shape=jax.ShapeDtypeStruct((bsz, rows, GRID_W, cw), BF16),
        scratch_shapes=[pltpu.VMEM((rows, W_PADDED, half), F32),
                        pltpu.VMEM((rows + 2 * W_OFF, GRID_W, half), F32)],
        compiler_params=_cparams("parallel"),
        name="conv",
    )(u4, taps, vec(bias), vec(ln_g), vec(ln_b))
    return out.reshape(bsz, n, cw)


def _scan_constants():
    t = np.arange(CHUNK)
    cum_f = (t[None, :] <= t[:, None]).astype(np.float32)
    signs, masks = [], []
    for level in range(N_LEVELS):
        s = CHUNK >> (level + 1)
        upper = (t // s) % 2 == 1
        same = (t[:, None] // (2 * s)) == (t[None, :] // (2 * s))
        masks.append(same & upper[:, None] & ~upper[None, :])
        signs.append(np.where(upper, 1.0, -1.0))
    masks.append(t[:, None] == t[None, :])
    sign_f = np.broadcast_to(np.stack(signs[:-1])[:, :, None], (N_LEVELS - 1, CHUNK, HEAD_DIM))
    pair_f = np.stack(masks).astype(np.float32)
    cum = np.stack([np.tile(m, (1, 3)) for m in (cum_f, cum_f[::-1, ::-1])])
    sign = np.stack([sign_f, sign_f[:, ::-1]])
    pair = np.stack([pair_f, pair_f[:, ::-1, ::-1]])
    return (jnp.asarray(cum, dtype=BF16), jnp.asarray(sign, dtype=F32),
            jnp.asarray(pair, dtype=F32))


def _level_factors(x, f, sign_ref, direction):
    dk = x.shape[1]
    factors = []
    for level in range(N_LEVELS - 1):
        s = CHUNK >> (level + 1)
        refs = [jnp.broadcast_to(x[r:r + 1, :], (2 * s, dk))
                for r in range(s - 1 + direction, CHUNK, 2 * s)]
        x_ref = refs[0] if len(refs) == 1 else jnp.concatenate(refs, axis=0)
        factors.append(jnp.exp((x - x_ref) * sign_ref[direction, level]))
    factors.append(f)
    return factors


def _scan_kernel(q_ref, v_ref, lff_ref, lfb_ref, gs_ref, s0_ref, ng_ref, m_ref, sg_ref, pm_ref,
                 o_ref, acc_ref, qe_ref, inc_ref, dec_ref, st_ref, *, unroll):
    n = q_ref.shape[1]
    n_chunks = n // CHUNK
    lf_refs = (lff_ref, lfb_ref)
    total_rows = (CHUNK - 1, 0)

    def local(i, carry):
        chunks = [i * unroll + u for u in range(unroll)]
        rows = [pl.ds(pl.multiple_of(c * CHUNK, CHUNK), CHUNK) for c in chunks]
        chains = [(u, d) for u in range(unroll) for d in range(2)]
        q = [q_ref[0, r, :].astype(F32) for r in rows]
        qb = [t.astype(BF16) for t in q]
        v = [v_ref[0, r, :] for r in rows]
        g, x = {}, {}
        for u, d in chains:
            g[u, d] = lf_refs[d][0, rows[u], :]
            g0 = g[u, d].astype(BF16)
            r1 = g[u, d] - g0.astype(F32)
            g1 = r1.astype(BF16)
            g2 = (r1 - g1.astype(F32)).astype(BF16)
            x[u, d] = jnp.dot(m_ref[d], jnp.concatenate([g0, g1, g2], axis=0),
                              preferred_element_type=F32)
        kb, ql, kl, kd = {}, {}, {}, {}
        for u, d in chains:
            xc = x[u, d]
            x_tot = xc[total_rows[d]:total_rows[d] + 1, :]
            f = jnp.exp(g[u, d])
            k = 1.0 - f
            kb[u, d] = k.astype(BF16)
            qe_ref[rows[u], d * HEAD_DIM:(d + 1) * HEAD_DIM] = (q[u] * jnp.exp(xc)).astype(BF16)
            factors = _level_factors(xc, f, sg_ref, d)
            ql[u, d] = [(q[u] * e).astype(BF16) for e in factors]
            kl[u, d] = [(k * e).astype(BF16) for e in factors[:-1]] + [kb[u, d]]
            kd[u, d] = (k * jnp.exp(x_tot - xc)).astype(BF16)
            dec_ref[d, chunks[u]] = jnp.broadcast_to(jnp.exp(x_tot), (SUBLANES, HEAD_DIM))
        scores = {}
        for u, d in chains:
            s_d = _dot_nt(qb[u], kb[u, d])
            scores[u, d] = jnp.where(pm_ref[d, N_LEVELS] > 0.5, s_d, 0.0)
        for level in range(N_LEVELS):
            for u, d in chains:
                s_l = _dot_nt(ql[u, d][level], kl[u, d][level])
                scores[u, d] = jnp.where(pm_ref[d, level] > 0.5, s_l, scores[u, d])
        for u in range(unroll):
            out = jnp.dot(scores[u, 0].astype(BF16), v[u], preferred_element_type=F32)
            out = out + jnp.dot(scores[u, 1].astype(BF16), v[u], preferred_element_type=F32)
            acc_ref[rows[u], :] = out
        for u, d in chains:
            inc_ref[d, chunks[u]] = _dot_tn(v[u], kd[u, d])
        return carry

    lax.fori_loop(0, n_chunks // unroll, local, 0)

    def recur(i, states):
        nxt = []
        for direction, c in ((0, i), (1, n_chunks - 1 - i)):
            st = states[direction]
            st_ref[c, :, direction * HEAD_DIM:(direction + 1) * HEAD_DIM] = st.astype(BF16)
            nxt.append(st * dec_ref[direction, c, 0:1, :] + inc_ref[direction, c])
        return tuple(nxt)

    lax.fori_loop(0, n_chunks, recur, (s0_ref[0, 0, 0], s0_ref[0, 1, 0]))

    wide = 2 * unroll

    def readout(i, carry):
        chunks = [i * wide + u for u in range(wide)]
        rows = [pl.ds(pl.multiple_of(c * CHUNK, CHUNK), CHUNK) for c in chunks]
        inter = [_dot_nt(qe_ref[rows[u], :], st_ref[chunks[u]]) for u in range(wide)]
        for u in range(wide):
            o = acc_ref[rows[u], :] + inter[u]
            y = o * lax.rsqrt(jnp.mean(o * o, axis=-1, keepdims=True) + EPS)
            o_ref[0, rows[u], :] = (y * ng_ref[...]
                                    * gs_ref[0, rows[u], :].astype(F32)).astype(BF16)
        return carry

    lax.fori_loop(0, n_chunks // wide, readout, 0)


def _hgrn_scan(q, v, lff, lfb, gs, states, norm_g):
    bsz, n, width = q.shape
    n_chunks = n // CHUNK
    m_const, sg_const, pm_const = _scan_constants()
    head = pl.BlockSpec((1, n, HEAD_DIM), lambda b, h: (b, 0, h))
    return pl.pallas_call(
        functools.partial(_scan_kernel, unroll=4),
        grid=(bsz, N_HEADS),
        in_specs=[head, head, head, head, head,
                  pl.BlockSpec((1, 2, 1, HEAD_DIM, HEAD_DIM), lambda b, h: (b, 0, h, 0, 0)),
                  pl.BlockSpec((1, HEAD_DIM), lambda b, h: (0, h)),
                  pl.BlockSpec(m_const.shape, lambda b, h: (0, 0, 0)),
                  pl.BlockSpec(sg_const.shape, lambda b, h: (0, 0, 0, 0)),
                  pl.BlockSpec(pm_const.shape, lambda b, h: (0, 0, 0, 0))],
        out_specs=head,
        out_shape=jax.ShapeDtypeStruct((bsz, n, width), BF16),
        scratch_shapes=[pltpu.VMEM((n, HEAD_DIM), F32),
                        pltpu.VMEM((n, 2 * HEAD_DIM), BF16),
                        pltpu.VMEM((2, n_chunks, HEAD_DIM, HEAD_DIM), F32),
                        pltpu.VMEM((2, n_chunks, SUBLANES, HEAD_DIM), F32),
                        pltpu.VMEM((n_chunks, HEAD_DIM, 2 * HEAD_DIM), BF16)],
        compiler_params=_cparams("parallel", "parallel"),
        name="scan",
    )(q, v, lff, lfb, gs, states, norm_g.reshape(1, width), m_const, sg_const, pm_const)


def _oproj_kernel(cv_ref, hg_ref, x_ref, mod_ref, w_ref, g2_ref, rw_ref,
                  x1_ref, h2t_ref, pr_ref):
    mix = jnp.dot(cv_ref[0], w_ref[0:CONV_WIDTH, :], preferred_element_type=F32)
    mix = mix + jnp.dot(hg_ref[0], w_ref[CONV_WIDTH:, :], preferred_element_type=F32)
    x1 = x_ref[0] + mod_ref[0, 2:3, :] * mix
    x1_ref[0] = x1
    h2 = _rms(x1, g2_ref[...]) * (1.0 + mod_ref[0, 4:5, :]) + mod_ref[0, 3:4, :]
    h2t_ref[0] = h2.T.astype(BF16)
    logits = lax.dot_general(rw_ref[...], h2, (((1,), (1,)), ((), ())),
                             precision=HIGHEST, preferred_element_type=F32)
    z = jnp.exp(logits - jnp.max(logits, axis=0, keepdims=True))
    pr_ref[0] = z / jnp.sum(z, axis=0, keepdims=True)


def _out_project(conv_out, hg, x, mod3, w_out, norm2_g, router_t, tn):
    bsz, n, d = x.shape
    tok = lambda b, j: (b, j, 0)
    return pl.pallas_call(
        _oproj_kernel,
        grid=(bsz, n // tn),
        in_specs=[pl.BlockSpec((1, tn, CONV_WIDTH), tok),
                  pl.BlockSpec((1, tn, HGRN_WIDTH), tok),
                  pl.BlockSpec((1, tn, d), tok),
                  pl.BlockSpec((1, N_MOD, d), lambda b, j: (b, 0, 0)),
                  pl.BlockSpec(w_out.shape, lambda b, j: (0, 0)),
                  pl.BlockSpec((1, d), lambda b, j: (0, 0)),
                  pl.BlockSpec(router_t.shape, lambda b, j: (0, 0))],
        out_specs=[pl.BlockSpec((1, tn, d), tok),
                   pl.BlockSpec((1, d, tn), lambda b, j: (b, 0, j)),
                   pl.BlockSpec((1, N_EXPERTS, tn), lambda b, j: (b, 0, j))],
        out_shape=[jax.ShapeDtypeStruct((bsz, n, d), F32),
                   jax.ShapeDtypeStruct((bsz, d, n), BF16),
                   jax.ShapeDtypeStruct((bsz, N_EXPERTS, n), F32)],
        compiler_params=_cparams("parallel", "parallel"),
        name="oproj",
    )(conv_out, hg, x, mod3, w_out, norm2_g, router_t)


def _prefix_count(flags, tri_ref):
    n = flags.shape[1]
    run = jnp.zeros((flags.shape[0], 1), F32)
    pieces = []
    for j in range(n // LANES):
        blk = flags[:, j * LANES:(j + 1) * LANES]
        inc = jnp.dot(blk.astype(BF16), tri_ref[...], preferred_element_type=F32)
        pieces.append(inc - blk + run)
        run = run + inc[:, LANES - 1:LANES]
    return jnp.concatenate(pieces, axis=1)


def _topk_kernel(p_ref, tri_ref, tile_ref, slot_ref, off_ref, *, cap):
    p = p_ref[0]
    n_exp = p.shape[0]
    capf = jnp.float32(cap)

    def count_ge(t):
        return jnp.sum(jnp.where(p >= t, 1.0, 0.0), axis=1, keepdims=True)

    def cond(state):
        return state[2] > 0

    def body(state):
        lo, hi, _ = state
        mid = 0.5 * (lo + hi)
        ge = count_ge(mid) >= capf
        lo_n = jnp.where(ge, mid, lo)
        hi_n = jnp.where(ge, hi, mid)
        nxt = 0.5 * (lo_n + hi_n)
        open_ = jnp.where((nxt > lo_n) & (nxt < hi_n), 1, 0)
        return lo_n, hi_n, jnp.max(open_)

    lo0 = jnp.zeros((n_exp, 1), p.dtype)
    hi0 = jnp.full((n_exp, 1), 2.0, p.dtype)
    thr, _, _ = lax.while_loop(cond, body, (lo0, hi0, jnp.int32(1)))
    gt = jnp.where(p > thr, 1.0, 0.0)
    eq = jnp.where(p == thr, 1.0, 0.0)
    need = capf - jnp.sum(gt, axis=1, keepdims=True)
    sel = gt + eq * jnp.where(_prefix_count(eq, tri_ref) < need, 1.0, 0.0)
    slot = _prefix_count(sel, tri_ref)
    slot_ref[0] = jnp.where(sel > 0.5, slot, -1.0).astype(jnp.int32)
    off_ref[0] = jnp.dot(sel.astype(BF16), tile_ref[...],
                         preferred_element_type=F32).astype(jnp.int32)


def _expert_choice(probs, cap, tt):
    bsz, n_exp, n = probs.shape
    idx = np.arange(LANES)
    tri = jnp.asarray((idx[:, None] <= idx[None, :]).astype(np.float32), dtype=BF16)
    assert n // tt < LANES
    tile = jnp.asarray((np.arange(n)[:, None] < idx[None, :] * tt).astype(np.float32), dtype=BF16)
    return pl.pallas_call(
        functools.partial(_topk_kernel, cap=cap),
        grid=(bsz,),
        in_specs=[pl.BlockSpec((1, n_exp, n), lambda b: (b, 0, 0)),
                  pl.BlockSpec((LANES, LANES), lambda b: (0, 0)),
                  pl.BlockSpec((n, LANES), lambda b: (0, 0))],
        out_specs=[pl.BlockSpec((1, n_exp, n), lambda b: (b, 0, 0)),
                   pl.BlockSpec((1, n_exp, LANES), lambda b: (b, 0, 0))],
        out_shape=[jax.ShapeDtypeStruct((bsz, n_exp, n), jnp.int32),
                   jax.ShapeDtypeStruct((bsz, n_exp, LANES), jnp.int32)],
        compiler_params=_cparams("parallel"),
        name="topk",
    )(probs, tri, tile)


def _one_hot_rows(slot_row, first, rows):
    ids = lax.broadcasted_iota(jnp.int32, (rows, slot_row.shape[1]), 0) + first
    return jnp.where(ids == slot_row, 1.0, 0.0).astype(BF16)


def _pair_table_scratch(n, tt, cap, ts):
    n_pairs = n // tt + cap // ts - 1
    return [pltpu.SMEM((n_pairs,), jnp.int32), pltpu.SMEM((n_pairs,), jnp.int32)]


def _band_pairs(off_ref, slot_ref, pj_ref, pi_ref, n_tt, tt, ts, cap):
    n_st = cap // ts
    n_pairs = n_tt + n_st - 1
    base = (pl.program_id(0) * pl.num_programs(1) + pl.program_id(1)) * (n_tt + 1)
    for p in range(n_pairs):
        pj_ref[p] = 0
        pi_ref[p] = n_st
    count = jnp.int32(0)
    for i in range(n_st):
        for j in range(n_tt):
            lo = off_ref[base + j]
            hi = off_ref[base + j + 1]
            hit = (hi > lo) & (hi > i * ts) & (lo < (i + 1) * ts)

            @pl.when(hit)
            def _(i=i, j=j, count=count):
                pj_ref[count] = j
                pi_ref[count] = i

            count = count + hit.astype(jnp.int32)
    pairs = []
    for p in range(n_pairs):
        i = pi_ref[p]
        cols = pl.ds(pl.multiple_of(pj_ref[p] * tt, tt), tt)
        tile = pl.ds(pl.multiple_of(jnp.minimum(i, n_st - 1) * ts, ts), ts)
        pairs.append((cols, tile, _one_hot_rows(slot_ref[0, 0, :, cols], i * ts, ts)))
    return pairs


def _ffn_kernel(off_ref, ht_ref, slot_ref, wg_ref, wu_ref, wd_ref, o_ref, xs_ref,
                pj_ref, pi_ref, *, cap, tt, ts):
    xs_ref[...] = jnp.zeros_like(xs_ref)
    for cols, tile, onehot in _band_pairs(off_ref, slot_ref, pj_ref, pi_ref,
                                          ht_ref.shape[2] // tt, tt, ts, cap):
        xs_ref[:, tile] += _dot_nt(ht_ref[0, :, cols], onehot)
    xs = xs_ref[...].astype(BF16)
    gate_t = _dot_tn(wg_ref[0], xs)
    up_t = _dot_tn(wu_ref[0], xs)
    hid_t = (_silu(gate_t) * up_t).astype(BF16)
    o_ref[0, 0] = _dot_tn(wd_ref[0], hid_t).astype(BF16)


def _expert_ffn(off, h2t, slot4, w_gate, w_up, w_down, cap, tt, ts):
    bsz, d, n = h2t.shape
    n_exp, _, ff = w_gate.shape
    return pl.pallas_call(
        functools.partial(_ffn_kernel, cap=cap, tt=tt, ts=ts),
        grid_spec=pltpu.PrefetchScalarGridSpec(
            num_scalar_prefetch=1,
            grid=(bsz, n_exp),
            in_specs=[pl.BlockSpec((1, d, n), lambda b, e, off: (b, 0, 0)),
                      pl.BlockSpec((1, 1, 1, n), lambda b, e, off: (b, e, 0, 0)),
                      pl.BlockSpec((1, d, ff), lambda b, e, off: (e, 0, 0)),
                      pl.BlockSpec((1, d, ff), lambda b, e, off: (e, 0, 0)),
                      pl.BlockSpec((1, ff, d), lambda b, e, off: (e, 0, 0))],
            out_specs=pl.BlockSpec((1, 1, d, cap), lambda b, e, off: (b, e, 0, 0)),
            scratch_shapes=[pltpu.VMEM((d, cap), F32)] + _pair_table_scratch(n, tt, cap, ts)),
        out_shape=jax.ShapeDtypeStruct((bsz, n_exp, d, cap), BF16),
        compiler_params=_cparams("parallel", "arbitrary"),
        name="ffn",
    )(off, h2t, slot4, w_gate, w_up, w_down)


def _comb_kernel(off_ref, out_ref, slot_ref, p_ref, y_ref, pj_ref, pi_ref, *, cap, tt, ts):
    @pl.when(pl.program_id(1) == 0)
    def _():
        y_ref[...] = jnp.zeros_like(y_ref)

    for cols, tile, onehot in _band_pairs(off_ref, slot_ref, pj_ref, pi_ref,
                                          y_ref.shape[2] // tt, tt, ts, cap):
        part = jnp.dot(out_ref[0, 0, :, tile], onehot, preferred_element_type=F32)
        y_ref[0, :, cols] = y_ref[0, :, cols] + part * p_ref[0, 0, :, cols]


def _combine(off, out_t, slot4, probs4, n, tt, ts):
    bsz, n_exp, d, cap = out_t.shape
    row = pl.BlockSpec((1, 1, 1, n), lambda b, e, off: (b, e, 0, 0))
    return pl.pallas_call(
        functools.partial(_comb_kernel, cap=cap, tt=tt, ts=ts),
        grid_spec=pltpu.PrefetchScalarGridSpec(
            num_scalar_prefetch=1,
            grid=(bsz, n_exp),
            in_specs=[pl.BlockSpec((1, 1, d, cap), lambda b, e, off: (b, e, 0, 0)), row, row],
            out_specs=pl.BlockSpec((1, d, n), lambda b, e, off: (b, 0, 0)),
            scratch_shapes=_pair_table_scratch(n, tt, cap, ts)),
        out_shape=jax.ShapeDtypeStruct((bsz, d, n), F32),
        compiler_params=_cparams("parallel", "arbitrary"),
        name="comb",
    )(off, out_t, slot4, probs4)


def _final_kernel(x1_ref, yt_ref, mod_ref, g_ref, o_ref):
    x2 = x1_ref[0] + mod_ref[0, 5:6, :] * yt_ref[0].T
    o_ref[0] = _rms(x2, g_ref[...])


def _final(x1, y_t, mod3, final_g, tn):
    bsz, n, d = x1.shape
    tok = lambda b, j: (b, j, 0)
    return pl.pallas_call(
        _final_kernel,
        grid=(bsz, n // tn),
        in_specs=[pl.BlockSpec((1, tn, d), tok),
                  pl.BlockSpec((1, d, tn), lambda b, j: (b, 0, j)),
                  pl.BlockSpec((1, N_MOD, d), lambda b, j: (b, 0, 0)),
                  pl.BlockSpec((1, d), lambda b, j: (0, 0))],
        out_specs=pl.BlockSpec((1, tn, d), tok),
        out_shape=jax.ShapeDtypeStruct((bsz, n, d), F32),
        compiler_params=_cparams("parallel", "parallel"),
        name="final",
    )(x1, y_t, mod3, final_g)


def kernel(x, c, ctx, c_ctx, ada_w, ada_b, norm1_g, w_in, conv_taps, conv_bias, conv_ln_g,
           conv_ln_b, hgrn_lb_logits, hgrn_norm_g, w_out, norm2_g, router_w, w_gate, w_up,
           w_down, final_g):
    bsz, n, d = x.shape
    assert ada_w.shape[0] == 1, "single-layer stack"
    assert n % GRID_W == 0 and n % CHUNK == 0 and ctx.shape[1] % SUBLANES == 0
    cap = CAPACITY_FACTOR * n // N_EXPERTS
    tn = min(512, n)

    rows = -(-(bsz + 1) // SUBLANES) * SUBLANES
    cvec = jnp.zeros((rows, d), F32).at[:bsz].set(c).at[rows - 1].set(c_ctx)
    mod3 = _modulation(cvec, ada_w[0], ada_b[0]).reshape(rows, N_MOD, d)

    w_in_b = w_in[0].astype(BF16)
    g1 = norm1_g[0].reshape(1, d)
    states = _ctx_states(ctx, mod3, g1, w_in_b[:, OFF_I:OFF_G], hgrn_lb_logits)

    u, q, v, lff, lfb, gs = _project(x, mod3, g1, w_in_b, hgrn_lb_logits, tn)
    conv_out = _conv_module(u, conv_taps[0], conv_bias[0], conv_ln_g[0], conv_ln_b[0])
    hg = _hgrn_scan(q, v, lff, lfb, gs, states, hgrn_norm_g[0])

    x1, h2t, probs = _out_project(conv_out, hg, x, mod3, w_out[0].astype(BF16),
                                  norm2_g[0].reshape(1, d), router_w[0].T, tn)
    tt = min(2 * MXU_DIM, n)
    ts = min(MXU_DIM, cap)
    slot, off = _expert_choice(probs, cap, tt)
    off = off[:, :, :n // tt + 1].reshape(-1)
    slot4 = slot.reshape(bsz, N_EXPERTS, 1, n)
    probs4 = probs.reshape(bsz, N_EXPERTS, 1, n)
    out_t = _expert_ffn(off, h2t, slot4, w_gate[0].astype(BF16), w_up[0].astype(BF16),
                        w_down[0].astype(BF16), cap, tt, ts)
    y_t = _combine(off, out_t, slot4, probs4, n, tt, ts)
    return _final(x1, y_t, mod3, final_g.reshape(1, d), tn)
```

```python
import functools

import jax
import jax.numpy as jnp
import numpy as np
from jax import lax
from jax.experimental import pallas as pl
from jax.experimental.pallas import tpu as pltpu

F32 = jnp.float32
BF16 = jnp.bfloat16
HIGHEST = lax.Precision.HIGHEST

CONV_WIDTH = 512
HGRN_WIDTH = 512
HEAD_DIM = 128
N_HEADS = HGRN_WIDTH // HEAD_DIM
CONV_K = 31
CONV_PAD = (CONV_K - 1) // 2
GRID_W = 64
CHUNK = 64
N_EXPERTS = 16
CAPACITY_FACTOR = 2
N_MOD = 6
EPS = 1e-6
OFF_CONV_B = CONV_WIDTH
OFF_Q = 2 * CONV_WIDTH
OFF_I = OFF_Q + HGRN_WIDTH
OFF_FF = OFF_I + HGRN_WIDTH
OFF_FB = OFF_FF + HGRN_WIDTH
OFF_G = OFF_FB + HGRN_WIDTH
IN_COLS = OFF_G + HGRN_WIDTH

SUBLANES = 8
LANES = 128
MXU_DIM = 256
VMEM_LIMIT = 56 * 1024 * 1024

N_LEVELS = 6


def _cparams(*sem, flags=None):
    return pltpu.CompilerParams(dimension_semantics=sem, vmem_limit_bytes=VMEM_LIMIT,
                                flags=flags)


def _sigmoid(x):
    return 1.0 / (1.0 + jnp.exp(-x))


def _silu(x):
    return x * _sigmoid(x)


def _rms(x, g):
    return x * lax.rsqrt(jnp.mean(x * x, axis=-1, keepdims=True) + EPS) * g


def _dot_nt(a, b):
    return lax.dot_general(a, b, (((1,), (1,)), ((), ())), preferred_element_type=F32)


def _dot_tn(a, b):
    return lax.dot_general(a, b, (((0,), (0,)), ((), ())), preferred_element_type=F32)


def _dot3(m_bf16, g):
    g0 = g.astype(BF16)
    r1 = g - g0.astype(F32)
    g1 = r1.astype(BF16)
    g2 = (r1 - g1.astype(F32)).astype(BF16)
    acc = jnp.dot(m_bf16, g0, preferred_element_type=F32)
    acc = acc + jnp.dot(m_bf16, g1, preferred_element_type=F32)
    return acc + jnp.dot(m_bf16, g2, preferred_element_type=F32)


def _lower_bound(lbl_ref, direction):
    l = lbl_ref[:, direction:direction + 1, :]
    m = jnp.max(l, axis=0)
    e = jnp.exp(l - m[None])
    return e[0] / jnp.sum(e, axis=0)


def _mod_kernel(cv_ref, w_ref, b_ref, o_ref):
    s = _silu(cv_ref[...])
    o_ref[...] = jnp.dot(s, w_ref[...], precision=HIGHEST,
                         preferred_element_type=F32) + b_ref[...]


def _modulation(cvec, w, b):
    rows, d = cvec.shape
    cols = w.shape[1]
    return pl.pallas_call(
        _mod_kernel,
        grid=(cols // d,),
        in_specs=[pl.BlockSpec((rows, d), lambda j: (0, 0)),
                  pl.BlockSpec((d, d), lambda j: (0, j)),
                  pl.BlockSpec((1, d), lambda j: (0, j))],
        out_specs=pl.BlockSpec((rows, d), lambda j: (0, j)),
        out_shape=jax.ShapeDtypeStruct((rows, cols), F32),
        compiler_params=_cparams("parallel"),
        name="mod",
    )(cvec, w, b.reshape(1, cols))


def _ctx_kernel(ctx_ref, mod_ref, g_ref, w_ref, lbl_ref, tri_ref, o_ref):
    x = ctx_ref[0]
    h = _rms(x, g_ref[...]) * (1.0 + mod_ref[0, 1:2, :]) + mod_ref[0, 0:1, :]
    p = jnp.dot(h.astype(BF16), w_ref[...], preferred_element_type=F32)
    v = p[:, :HGRN_WIDTH].astype(BF16)
    for direction in range(2):
        z = p[:, (1 + direction) * HGRN_WIDTH:(2 + direction) * HGRN_WIDTH]
        lb = _lower_bound(lbl_ref, direction)
        f = lb + (1.0 - lb) * _sigmoid(z)
        expo = _dot3(tri_ref[direction], jnp.log(f))
        kd = ((1.0 - f) * jnp.exp(expo)).astype(BF16)
        for hd in range(N_HEADS):
            sl = slice(hd * HEAD_DIM, (hd + 1) * HEAD_DIM)
            o_ref[0, direction, hd] = _dot_tn(v[:, sl], kd[:, sl])


def _ctx_states(ctx, mod3, norm_g, w_ctx, lb_logits):
    bsz, length, d = ctx.shape
    idx = np.arange(length)
    tri = np.stack([idx[None, :] > idx[:, None], idx[None, :] < idx[:, None]])
    tri = jnp.asarray(tri.astype(np.float32), dtype=BF16)
    ctx_row = mod3.shape[0] - 1
    return pl.pallas_call(
        _ctx_kernel,
        grid=(bsz,),
        in_specs=[pl.BlockSpec((1, length, d), lambda b: (b, 0, 0)),
                  pl.BlockSpec((1, N_MOD, d), lambda b: (ctx_row, 0, 0)),
                  pl.BlockSpec((1, d), lambda b: (0, 0)),
                  pl.BlockSpec(w_ctx.shape, lambda b: (0, 0)),
                  pl.BlockSpec(lb_logits.shape, lambda b: (0, 0, 0)),
                  pl.BlockSpec(tri.shape, lambda b: (0, 0, 0))],
        out_specs=pl.BlockSpec((1, 2, N_HEADS, HEAD_DIM, HEAD_DIM), lambda b: (b, 0, 0, 0, 0)),
        out_shape=jax.ShapeDtypeStruct((bsz, 2, N_HEADS, HEAD_DIM, HEAD_DIM), F32),
        compiler_params=_cparams("parallel"),
        name="ctx",
    )(ctx, mod3, norm_g, w_ctx, lb_logits, tri)


def _proj_kernel(x_ref, mod_ref, g_ref, w_ref, lbl_ref,
                 u_ref, q_ref, v_ref, lff_ref, lfb_ref, gs_ref):
    x = x_ref[0]
    h = _rms(x, g_ref[...]) * (1.0 + mod_ref[0, 1:2, :]) + mod_ref[0, 0:1, :]
    hb = h.astype(BF16)

    def cols(off, width):
        return jnp.dot(hb, w_ref[:, off:off + width], preferred_element_type=F32)

    u_ref[0] = (cols(0, CONV_WIDTH) * _sigmoid(cols(OFF_CONV_B, CONV_WIDTH))).astype(BF16)
    for direction, (off, out) in enumerate(((OFF_FF, lff_ref), (OFF_FB, lfb_ref))):
        lb = _lower_bound(lbl_ref, direction)
        out[0] = jnp.log2(lb + (1.0 - lb) * _sigmoid(cols(off, HGRN_WIDTH)))
    gs_ref[0] = _silu(cols(OFF_G, HGRN_WIDTH)).astype(BF16)
    q_ref[0] = cols(OFF_Q, HGRN_WIDTH).astype(BF16)
    v_ref[0] = cols(OFF_I, HGRN_WIDTH).astype(BF16)


def _project(x, mod3, norm_g, w_in, lb_logits, tn):
    bsz, n, d = x.shape
    tok = lambda b, j: (b, j, 0)
    out_block = pl.BlockSpec((1, tn, HGRN_WIDTH), tok)
    sds = lambda dt: jax.ShapeDtypeStruct((bsz, n, HGRN_WIDTH), dt)
    return pl.pallas_call(
        _proj_kernel,
        grid=(bsz, n // tn),
        in_specs=[pl.BlockSpec((1, tn, d), tok),
                  pl.BlockSpec((1, N_MOD, d), lambda b, j: (b, 0, 0)),
                  pl.BlockSpec((1, d), lambda b, j: (0, 0)),
                  pl.BlockSpec(w_in.shape, lambda b, j: (0, 0)),
                  pl.BlockSpec(lb_logits.shape, lambda b, j: (0, 0, 0))],
        out_specs=[out_block] * 6,
        out_shape=[sds(BF16), sds(BF16), sds(BF16), sds(F32), sds(F32), sds(BF16)],
        compiler_params=_cparams("parallel", "parallel"),
        name="proj",
    )(x, mod3, norm_g, w_in, lb_logits)


W_OFF = 16
W_PADDED = GRID_W + 2 * W_OFF


def _conv_kernel(u_ref, taps_ref, bias_ref, lng_ref, lnb_ref, o_ref, hbuf, vbuf):
    rows = u_ref.shape[1]
    half = CONV_WIDTH // 2
    zeros_h = jnp.zeros((rows, W_OFF, half), F32)
    hbuf[:, 0:W_OFF, :] = zeros_h
    hbuf[:, W_OFF + GRID_W:W_PADDED, :] = zeros_h
    hbuf[:, W_OFF:W_OFF + GRID_W, :] = u_ref[0, :, :, 0:half].astype(F32)
    zeros_v = jnp.zeros((W_OFF, GRID_W, half), F32)
    vbuf[0:W_OFF] = zeros_v
    vbuf[W_OFF + rows:W_OFF + rows + W_OFF] = zeros_v
    vbuf[W_OFF:W_OFF + rows] = u_ref[0, :, :, half:CONV_WIDTH].astype(F32)

    def row_body(r, carry):
        acc_h = jnp.zeros((GRID_W, half), F32)
        span = GRID_W + SUBLANES
        for b in range(SUBLANES):
            part = None
            for k in range(b - 1, CONV_K, SUBLANES):
                if k < 0:
                    continue
                start = k + 1 - b
                term = hbuf[r, start:start + span, :] * taps_ref[k:k + 1, 0:half]
                part = term if part is None else part + term
            acc_h = acc_h + part[b:b + GRID_W]
        acc_v = jnp.zeros((GRID_W, half), F32)
        for k in range(CONV_K):
            shift = W_OFF + k - CONV_PAD
            acc_v = acc_v + vbuf[r + shift] * taps_ref[k:k + 1, half:CONV_WIDTH]
        acc_h = acc_h + bias_ref[:, 0:half]
        acc_v = acc_v + bias_ref[:, half:CONV_WIDTH]
        mu = (jnp.sum(acc_h, axis=-1, keepdims=True)
              + jnp.sum(acc_v, axis=-1, keepdims=True)) / CONV_WIDTH
        ch = acc_h - mu
        cv = acc_v - mu
        var = (jnp.sum(ch * ch, axis=-1, keepdims=True)
               + jnp.sum(cv * cv, axis=-1, keepdims=True)) / CONV_WIDTH
        rs = lax.rsqrt(var + EPS)
        yh = ch * rs * lng_ref[:, 0:half] + lnb_ref[:, 0:half]
        yv = cv * rs * lng_ref[:, half:CONV_WIDTH] + lnb_ref[:, half:CONV_WIDTH]
        o_ref[0, r, :, 0:half] = _silu(yh).astype(BF16)
        o_ref[0, r, :, half:CONV_WIDTH] = _silu(yv).astype(BF16)
        return carry

    lax.fori_loop(0, rows, row_body, 0)


def _conv_module(u, taps, bias, ln_g, ln_b):
    bsz, n, cw = u.shape
    rows = n // GRID_W
    half = cw // 2
    u4 = u.reshape(bsz, rows, GRID_W, cw)
    vec = lambda a: a.reshape(1, cw)
    out = pl.pallas_call(
        _conv_kernel,
        grid=(bsz,),
        in_specs=[pl.BlockSpec((1, rows, GRID_W, cw), lambda b: (b, 0, 0, 0)),
                  pl.BlockSpec((CONV_K, cw), lambda b: (0, 0)),
                  pl.BlockSpec((1, cw), lambda b: (0, 0)),
                  pl.BlockSpec((1, cw), lambda b: (0, 0)),
                  pl.BlockSpec((1, cw), lambda b: (0, 0))],
        out_specs=pl.BlockSpec((1, rows, GRID_W, cw), lambda b: (b, 0, 0, 0)),
        out_shape=jax.ShapeDtypeStruct((bsz, rows, GRID_W, cw), BF16),
        scratch_shapes=[pltpu.VMEM((rows, W_PADDED, half), F32),
                        pltpu.VMEM((rows + 2 * W_OFF, GRID_W, half), F32)],
        compiler_params=_cparams("parallel"),
        name="conv",
    )(u4, taps, vec(bias), vec(ln_g), vec(ln_b))
    return out.reshape(bsz, n, cw)


def _scan_constants():
    t = np.arange(CHUNK)
    cum_f = (t[None, :] <= t[:, None]).astype(np.float32)
    signs, masks = [], []
    for level in range(N_LEVELS):
        s = CHUNK >> (level + 1)
        upper = (t // s) % 2 == 1
        same = (t[:, None] // (2 * s)) == (t[None, :] // (2 * s))
        masks.append(same & upper[:, None] & ~upper[None, :])
        signs.append(np.where(upper, 1.0, -1.0))
    masks.append(t[:, None] == t[None, :])
    sign_f = np.broadcast_to(np.stack(signs[:-1])[:, :, None], (N_LEVELS - 1, CHUNK, HEAD_DIM))
    pair_f = np.stack(masks).astype(np.float32)
    cum = np.stack([np.tile(m, (1, 3)) for m in (cum_f, cum_f[::-1, ::-1])])
    sign = np.stack([sign_f, sign_f[:, ::-1]])
    pair = np.stack([pair_f, pair_f[:, ::-1, ::-1]])
    return (jnp.asarray(cum, dtype=BF16), jnp.asarray(sign, dtype=F32),
            jnp.asarray(pair, dtype=F32))


def _level_factors(x, f, sign_ref, direction):
    dk = x.shape[1]
    factors = []
    for level in range(N_LEVELS - 1):
        s = CHUNK >> (level + 1)
        refs = [jnp.broadcast_to(x[r:r + 1, :], (2 * s, dk))
                for r in range(s - 1 + direction, CHUNK, 2 * s)]
        x_ref = refs[0] if len(refs) == 1 else jnp.concatenate(refs, axis=0)
        factors.append(jnp.exp2((x - x_ref) * sign_ref[direction, level]))
    factors.append(f)
    return factors


def _scan_kernel(q_ref, v_ref, lff_ref, lfb_ref, gs_ref, s0_ref, ng_ref, m_ref, sg_ref, pm_ref,
                 o_ref, acc_ref, qe_ref, inc_ref, dec_ref, st_ref, *, unroll):
    n = q_ref.shape[1]
    n_chunks = n // CHUNK
    lf_refs = (lff_ref, lfb_ref)
    total_rows = (CHUNK - 1, 0)

    def local(i, carry):
        chunks = [i * unroll + u for u in range(unroll)]
        rows = [pl.ds(pl.multiple_of(c * CHUNK, CHUNK), CHUNK) for c in chunks]
        chains = [(u, d) for u in range(unroll) for d in range(2)]
        q = [q_ref[0, r, :].astype(F32) for r in rows]
        qb = [t.astype(BF16) for t in q]
        v = [v_ref[0, r, :] for r in rows]
        g, x = {}, {}
        for u, d in chains:
            g[u, d] = lf_refs[d][0, rows[u], :]
            g0 = g[u, d].astype(BF16)
            r1 = g[u, d] - g0.astype(F32)
            g1 = r1.astype(BF16)
            g2 = (r1 - g1.astype(F32)).astype(BF16)
            x[u, d] = jnp.dot(m_ref[d], jnp.concatenate([g0, g1, g2], axis=0),
                              preferred_element_type=F32)
        kb, ql, kl, kd = {}, {}, {}, {}
        for u, d in chains:
            xc = x[u, d]
            x_tot = xc[total_rows[d]:total_rows[d] + 1, :]
            f = jnp.exp2(g[u, d])
            k = 1.0 - f
            kb[u, d] = k.astype(BF16)
            qe_ref[rows[u], d * HEAD_DIM:(d + 1) * HEAD_DIM] = (q[u] * jnp.exp2(xc)).astype(BF16)
            factors = _level_factors(xc, f, sg_ref, d)
            ql[u, d] = [(q[u] * e).astype(BF16) for e in factors]
            kl[u, d] = [(k * e).astype(BF16) for e in factors[:-1]] + [kb[u, d]]
            kd[u, d] = (k * jnp.exp2(x_tot - xc)).astype(BF16)
            dec_ref[d, chunks[u]] = jnp.broadcast_to(jnp.exp2(x_tot), (SUBLANES, HEAD_DIM))
        scores = {}
        for u, d in chains:
            s_d = _dot_nt(qb[u], kb[u, d])
            scores[u, d] = jnp.where(pm_ref[d, N_LEVELS] > 0.5, s_d, 0.0)
        for level in range(N_LEVELS):
            for u, d in chains:
                s_l = _dot_nt(ql[u, d][level], kl[u, d][level])
                scores[u, d] = jnp.where(pm_ref[d, level] > 0.5, s_l, scores[u, d])
        for u in range(unroll):
            out = jnp.dot(scores[u, 0].astype(BF16), v[u], preferred_element_type=F32)
            out = out + jnp.dot(scores[u, 1].astype(BF16), v[u], preferred_element_type=F32)
            acc_ref[rows[u], :] = out
        for u, d in chains:
            inc_ref[d, chunks[u]] = _dot_tn(v[u], kd[u, d])
        return carry

    lax.fori_loop(0, n_chunks // unroll, local, 0)

    def recur(i, states):
        nxt = []
        for direction, c in ((0, i), (1, n_chunks - 1 - i)):
            st = states[direction]
            st_ref[c, :, direction * HEAD_DIM:(direction + 1) * HEAD_DIM] = st.astype(BF16)
            nxt.append(st * dec_ref[direction, c, 0:1, :] + inc_ref[direction, c])
        return tuple(nxt)

    lax.fori_loop(0, n_chunks, recur, (s0_ref[0, 0, 0], s0_ref[0, 1, 0]))

    wide = 8

    def readout(i, carry):
        chunks = [i * wide + u for u in range(wide)]
        rows = [pl.ds(pl.multiple_of(c * CHUNK, CHUNK), CHUNK) for c in chunks]
        inter = [_dot_nt(qe_ref[rows[u], :], st_ref[chunks[u]]) for u in range(wide)]
        for u in range(wide):
            o = acc_ref[rows[u], :] + inter[u]
            y = o * lax.rsqrt(jnp.mean(o * o, axis=-1, keepdims=True) + EPS)
            o_ref[0, rows[u], :] = (y * ng_ref[...]
                                    * gs_ref[0, rows[u], :].astype(F32)).astype(BF16)
        return carry

    lax.fori_loop(0, n_chunks // wide, readout, 0)


def _hgrn_scan(q, v, lff, lfb, gs, states, norm_g):
    bsz, n, width = q.shape
    n_chunks = n // CHUNK
    m_const, sg_const, pm_const = _scan_constants()
    head = pl.BlockSpec((1, n, HEAD_DIM), lambda b, h: (b, 0, h))
    return pl.pallas_call(
        functools.partial(_scan_kernel, unroll=8),
        grid=(bsz, N_HEADS),
        in_specs=[head, head, head, head, head,
                  pl.BlockSpec((1, 2, 1, HEAD_DIM, HEAD_DIM), lambda b, h: (b, 0, h, 0, 0)),
                  pl.BlockSpec((1, HEAD_DIM), lambda b, h: (0, h)),
                  pl.BlockSpec(m_const.shape, lambda b, h: (0, 0, 0)),
                  pl.BlockSpec(sg_const.shape, lambda b, h: (0, 0, 0, 0)),
                  pl.BlockSpec(pm_const.shape, lambda b, h: (0, 0, 0, 0))],
        out_specs=head,
        out_shape=jax.ShapeDtypeStruct((bsz, n, width), BF16),
        scratch_shapes=[pltpu.VMEM((n, HEAD_DIM), F32),
                        pltpu.VMEM((n, 2 * HEAD_DIM), BF16),
                        pltpu.VMEM((2, n_chunks, HEAD_DIM, HEAD_DIM), F32),
                        pltpu.VMEM((2, n_chunks, SUBLANES, HEAD_DIM), F32),
                        pltpu.VMEM((n_chunks, HEAD_DIM, 2 * HEAD_DIM), BF16)],
        compiler_params=_cparams("parallel", "parallel"),
        name="scan",
    )(q, v, lff, lfb, gs, states, norm_g.reshape(1, width), m_const, sg_const, pm_const)


def _oproj_kernel(cv_ref, hg_ref, x_ref, mod_ref, w_ref, g2_ref, rw_ref,
                  x1_ref, h2t_ref, pr_ref):
    mix = jnp.dot(cv_ref[0], w_ref[0:CONV_WIDTH, :], preferred_element_type=F32)
    mix = mix + jnp.dot(hg_ref[0], w_ref[CONV_WIDTH:, :], preferred_element_type=F32)
    x1 = x_ref[0] + mod_ref[0, 2:3, :] * mix
    x1_ref[0] = x1
    h2 = _rms(x1, g2_ref[...]) * (1.0 + mod_ref[0, 4:5, :]) + mod_ref[0, 3:4, :]
    h2t_ref[0] = h2.T.astype(BF16)
    h_hi = h2.astype(BF16)
    h_lo = (h2 - h_hi.astype(F32)).astype(BF16)
    r_hi = rw_ref[...].astype(BF16)
    r_lo = (rw_ref[...] - r_hi.astype(F32)).astype(BF16)
    logits = _dot_nt(r_hi, h_hi) + _dot_nt(r_lo, h_hi) + _dot_nt(r_hi, h_lo)
    z = jnp.exp(logits - jnp.max(logits, axis=0, keepdims=True))
    pr_ref[0] = z / jnp.sum(z, axis=0, keepdims=True)


def _out_project(conv_out, hg, x, mod3, w_out, norm2_g, router_t, tn):
    bsz, n, d = x.shape
    tok = lambda b, j: (b, j, 0)
    return pl.pallas_call(
        _oproj_kernel,
        grid=(bsz, n // tn),
        in_specs=[pl.BlockSpec((1, tn, CONV_WIDTH), tok),
                  pl.BlockSpec((1, tn, HGRN_WIDTH), tok),
                  pl.BlockSpec((1, tn, d), tok),
                  pl.BlockSpec((1, N_MOD, d), lambda b, j: (b, 0, 0)),
                  pl.BlockSpec(w_out.shape, lambda b, j: (0, 0)),
                  pl.BlockSpec((1, d), lambda b, j: (0, 0)),
                  pl.BlockSpec(router_t.shape, lambda b, j: (0, 0))],
        out_specs=[pl.BlockSpec((1, tn, d), tok),
                   pl.BlockSpec((1, d, tn), lambda b, j: (b, 0, j)),
                   pl.BlockSpec((1, N_EXPERTS, tn), lambda b, j: (b, 0, j))],
        out_shape=[jax.ShapeDtypeStruct((bsz, n, d), F32),
                   jax.ShapeDtypeStruct((bsz, d, n), BF16),
                   jax.ShapeDtypeStruct((bsz, N_EXPERTS, n), F32)],
        compiler_params=_cparams("parallel", "parallel"),
        name="oproj",
    )(conv_out, hg, x, mod3, w_out, norm2_g, router_t)


def _prefix_count(flags, tri_ref):
    n = flags.shape[1]
    run = jnp.zeros((flags.shape[0], 1), F32)
    pieces = []
    for j in range(n // LANES):
        blk = flags[:, j * LANES:(j + 1) * LANES]
        inc = jnp.dot(blk.astype(BF16), tri_ref[...], preferred_element_type=F32)
        pieces.append(inc - blk + run)
        run = run + inc[:, LANES - 1:LANES]
    return jnp.concatenate(pieces, axis=1)


def _topk_kernel(p_ref, tri_ref, tile_ref, slot_ref, off_ref, *, cap):
    p = p_ref[0]
    n_exp = p.shape[0]
    capf = jnp.float32(cap)

    def count_ge(t):
        return jnp.sum(jnp.where(p >= t, 1.0, 0.0), axis=1, keepdims=True)

    def cond(state):
        return state[2] > 0

    def body(state):
        lo, hi, _ = state
        mid = 0.5 * (lo + hi)
        ge = count_ge(mid) >= capf
        lo_n = jnp.where(ge, mid, lo)
        hi_n = jnp.where(ge, hi, mid)
        nxt = 0.5 * (lo_n + hi_n)
        open_ = jnp.where((nxt > lo_n) & (nxt < hi_n), 1, 0)
        return lo_n, hi_n, jnp.max(open_)

    lo0 = jnp.zeros((n_exp, 1), p.dtype)
    hi0 = jnp.full((n_exp, 1), 2.0, p.dtype)
    thr, _, _ = lax.while_loop(cond, body, (lo0, hi0, jnp.int32(1)))
    gt = jnp.where(p > thr, 1.0, 0.0)
    eq = jnp.where(p == thr, 1.0, 0.0)
    need = capf - jnp.sum(gt, axis=1, keepdims=True)
    sel = gt + eq * jnp.where(_prefix_count(eq, tri_ref) < need, 1.0, 0.0)
    slot = _prefix_count(sel, tri_ref)
    slot_ref[0] = jnp.where(sel > 0.5, slot, -1.0).astype(jnp.int32)
    off_ref[0] = jnp.dot(sel.astype(BF16), tile_ref[...],
                         preferred_element_type=F32).astype(jnp.int32)


def _expert_choice(probs, cap, tt):
    bsz, n_exp, n = probs.shape
    idx = np.arange(LANES)
    tri = jnp.asarray((idx[:, None] <= idx[None, :]).astype(np.float32), dtype=BF16)
    assert n // tt < LANES
    tile = jnp.asarray((np.arange(n)[:, None] < idx[None, :] * tt).astype(np.float32), dtype=BF16)
    return pl.pallas_call(
        functools.partial(_topk_kernel, cap=cap),
        grid=(bsz,),
        in_specs=[pl.BlockSpec((1, n_exp, n), lambda b: (b, 0, 0)),
                  pl.BlockSpec((LANES, LANES), lambda b: (0, 0)),
                  pl.BlockSpec((n, LANES), lambda b: (0, 0))],
        out_specs=[pl.BlockSpec((1, n_exp, n), lambda b: (b, 0, 0)),
                   pl.BlockSpec((1, n_exp, LANES), lambda b: (b, 0, 0))],
        out_shape=[jax.ShapeDtypeStruct((bsz, n_exp, n), jnp.int32),
                   jax.ShapeDtypeStruct((bsz, n_exp, LANES), jnp.int32)],
        compiler_params=_cparams("parallel"),
        name="topk",
    )(probs, tri, tile)


def _one_hot_rows(slot_row, first, rows):
    ids = lax.broadcasted_iota(jnp.int32, (rows, slot_row.shape[1]), 0) + first
    return jnp.where(ids == slot_row, 1.0, 0.0).astype(BF16)


def _pair_table_scratch(n, tt, cap, ts):
    n_pairs = n // tt + cap // ts - 1
    return [pltpu.SMEM((n_pairs,), jnp.int32), pltpu.SMEM((n_pairs,), jnp.int32)]


def _band_pairs(off_ref, slot_ref, pj_ref, pi_ref, n_tt, tt, ts, cap):
    n_st = cap // ts
    n_pairs = n_tt + n_st - 1
    base = (pl.program_id(0) * pl.num_programs(1) + pl.program_id(1)) * (n_tt + 1)
    for p in range(n_pairs):
        pj_ref[p] = 0
        pi_ref[p] = n_st
    count = jnp.int32(0)
    for i in range(n_st):
        for j in range(n_tt):
            lo = off_ref[base + j]
            hi = off_ref[base + j + 1]
            hit = (hi > lo) & (hi > i * ts) & (lo < (i + 1) * ts)

            @pl.when(hit)
            def _(i=i, j=j, count=count):
                pj_ref[count] = j
                pi_ref[count] = i

            count = count + hit.astype(jnp.int32)
    pairs = []
    for p in range(n_pairs):
        i = pi_ref[p]
        cols = pl.ds(pl.multiple_of(pj_ref[p] * tt, tt), tt)
        tile = pl.ds(pl.multiple_of(jnp.minimum(i, n_st - 1) * ts, ts), ts)
        pairs.append((cols, tile, _one_hot_rows(slot_ref[0, 0, :, cols], i * ts, ts)))
    return pairs


def _ffn_kernel(off_ref, ht_ref, slot_ref, wg_ref, wu_ref, wd_ref, o_ref, xs_ref,
                pj_ref, pi_ref, *, cap, tt, ts):
    xs_ref[...] = jnp.zeros_like(xs_ref)
    for cols, tile, onehot in _band_pairs(off_ref, slot_ref, pj_ref, pi_ref,
                                          ht_ref.shape[2] // tt, tt, ts, cap):
        xs_ref[:, tile] += _dot_nt(ht_ref[0, :, cols], onehot)
    xs = xs_ref[...].astype(BF16)
    gate_t = _dot_tn(wg_ref[0], xs)
    up_t = _dot_tn(wu_ref[0], xs)
    hid_t = (_silu(gate_t) * up_t).astype(BF16)
    o_ref[0, 0] = _dot_tn(wd_ref[0], hid_t).astype(BF16)


def _expert_ffn(off, h2t, slot4, w_gate, w_up, w_down, cap, tt, ts):
    bsz, d, n = h2t.shape
    n_exp, _, ff = w_gate.shape
    return pl.pallas_call(
        functools.partial(_ffn_kernel, cap=cap, tt=tt, ts=ts),
        grid_spec=pltpu.PrefetchScalarGridSpec(
            num_scalar_prefetch=1,
            grid=(bsz, n_exp),
            in_specs=[pl.BlockSpec((1, d, n), lambda b, e, off: (b, 0, 0)),
                      pl.BlockSpec((1, 1, 1, n), lambda b, e, off: (b, e, 0, 0)),
                      pl.BlockSpec((1, d, ff), lambda b, e, off: (e, 0, 0)),
                      pl.BlockSpec((1, d, ff), lambda b, e, off: (e, 0, 0)),
                      pl.BlockSpec((1, ff, d), lambda b, e, off: (e, 0, 0))],
            out_specs=pl.BlockSpec((1, 1, d, cap), lambda b, e, off: (b, e, 0, 0)),
            scratch_shapes=[pltpu.VMEM((d, cap), F32)] + _pair_table_scratch(n, tt, cap, ts)),
        out_shape=jax.ShapeDtypeStruct((bsz, n_exp, d, cap), BF16),
        compiler_params=_cparams("parallel", "arbitrary"),
        name="ffn",
    )(off, h2t, slot4, w_gate, w_up, w_down)


def _comb_kernel(off_ref, out_ref, slot_ref, p_ref, y_ref, pj_ref, pi_ref, *, cap, tt, ts):
    @pl.when(pl.program_id(1) == 0)
    def _():
        y_ref[...] = jnp.zeros_like(y_ref)

    for cols, tile, onehot in _band_pairs(off_ref, slot_ref, pj_ref, pi_ref,
                                          y_ref.shape[2] // tt, tt, ts, cap):
        part = jnp.dot(out_ref[0, 0, :, tile], onehot, preferred_element_type=F32)
        y_ref[0, :, cols] = y_ref[0, :, cols] + part * p_ref[0, 0, :, cols]


def _combine(off, out_t, slot4, probs4, n, tt, ts):
    bsz, n_exp, d, cap = out_t.shape
    row = pl.BlockSpec((1, 1, 1, n), lambda b, e, off: (b, e, 0, 0))
    return pl.pallas_call(
        functools.partial(_comb_kernel, cap=cap, tt=tt, ts=ts),
        grid_spec=pltpu.PrefetchScalarGridSpec(
            num_scalar_prefetch=1,
            grid=(bsz, n_exp),
            in_specs=[pl.BlockSpec((1, 1, d, cap), lambda b, e, off: (b, e, 0, 0)), row, row],
            out_specs=pl.BlockSpec((1, d, n), lambda b, e, off: (b, 0, 0)),
            scratch_shapes=_pair_table_scratch(n, tt, cap, ts)),
        out_shape=jax.ShapeDtypeStruct((bsz, d, n), F32),
        compiler_params=_cparams("parallel", "arbitrary"),
        name="comb",
    )(off, out_t, slot4, probs4)


def _final_kernel(x1_ref, yt_ref, mod_ref, g_ref, o_ref):
    x2 = x1_ref[0] + mod_ref[0, 5:6, :] * yt_ref[0].T
    o_ref[0] = _rms(x2, g_ref[...])


def _final(x1, y_t, mod3, final_g, tn):
    bsz, n, d = x1.shape
    tok = lambda b, j: (b, j, 0)
    return pl.pallas_call(
        _final_kernel,
        grid=(bsz, n // tn),
        in_specs=[pl.BlockSpec((1, tn, d), tok),
                  pl.BlockSpec((1, d, tn), lambda b, j: (b, 0, j)),
                  pl.BlockSpec((1, N_MOD, d), lambda b, j: (b, 0, 0)),
                  pl.BlockSpec((1, d), lambda b, j: (0, 0))],
        out_specs=pl.BlockSpec((1, tn, d), tok),
        out_shape=jax.ShapeDtypeStruct((bsz, n, d), F32),
        compiler_params=_cparams("parallel", "parallel"),
        name="final",
    )(x1, y_t, mod3, final_g)


def kernel(x, c, ctx, c_ctx, ada_w, ada_b, norm1_g, w_in, conv_taps, conv_bias, conv_ln_g,
           conv_ln_b, hgrn_lb_logits, hgrn_norm_g, w_out, norm2_g, router_w, w_gate, w_up,
           w_down, final_g):
    bsz, n, d = x.shape
    assert ada_w.shape[0] == 1, "single-layer stack"
    assert n % GRID_W == 0 and n % CHUNK == 0 and ctx.shape[1] % SUBLANES == 0
    cap = CAPACITY_FACTOR * n // N_EXPERTS
    tn = min(512, n)

    rows = -(-(bsz + 1) // SUBLANES) * SUBLANES
    cvec = jnp.zeros((rows, d), F32).at[:bsz].set(c).at[rows - 1].set(c_ctx)
    mod3 = _modulation(cvec, ada_w[0], ada_b[0]).reshape(rows, N_MOD, d)

    w_in_b = w_in[0].astype(BF16)
    g1 = norm1_g[0].reshape(1, d)
    states = _ctx_states(ctx, mod3, g1, w_in_b[:, OFF_I:OFF_G], hgrn_lb_logits)

    u, q, v, lff, lfb, gs = _project(x, mod3, g1, w_in_b, hgrn_lb_logits, tn)
    conv_out = _conv_module(u, conv_taps[0], conv_bias[0], conv_ln_g[0], conv_ln_b[0])
    hg = _hgrn_scan(q, v, lff, lfb, gs, states, hgrn_norm_g[0])

    x1, h2t, probs = _out_project(conv_out, hg, x, mod3, w_out[0].astype(BF16),
                                  norm2_g[0].reshape(1, d), router_w[0].T, tn)
    tt = min(2 * MXU_DIM, n)
    ts = min(MXU_DIM, cap)
    slot, off = _expert_choice(probs, cap, tt)
    off = off[:, :, :n // tt + 1].reshape(-1)
    slot4 = slot.reshape(bsz, N_EXPERTS, 1, n)
    probs4 = probs.reshape(bsz, N_EXPERTS, 1, n)
    out_t = _expert_ffn(off, h2t, slot4, w_gate[0].astype(BF16), w_up[0].astype(BF16),
                        w_down[0].astype(BF16), cap, tt, ts)
    y_t = _combine(off, out_t, slot4, probs4, n, tt, ts)
    return _final(x1, y_t, mod3, final_g.reshape(1, d), tn)
```

```python
import functools

import jax
import jax.numpy as jnp
import numpy as np
from jax import lax
from jax.experimental import pallas as pl
from jax.experimental.pallas import tpu as pltpu

F32 = jnp.float32
BF16 = jnp.bfloat16
HIGHEST = lax.Precision.HIGHEST

CONV_WIDTH = 512
HGRN_WIDTH = 512
HEAD_DIM = 128
N_HEADS = HGRN_WIDTH // HEAD_DIM
CONV_K = 31
CONV_PAD = (CONV_K - 1) // 2
GRID_W = 64
CHUNK = 64
N_EXPERTS = 16
CAPACITY_FACTOR = 2
N_MOD = 6
EPS = 1e-6
OFF_CONV_B = CONV_WIDTH
OFF_Q = 2 * CONV_WIDTH
OFF_I = OFF_Q + HGRN_WIDTH
OFF_FF = OFF_I + HGRN_WIDTH
OFF_FB = OFF_FF + HGRN_WIDTH
OFF_G = OFF_FB + HGRN_WIDTH
IN_COLS = OFF_G + HGRN_WIDTH

SUBLANES = 8
LANES = 128
MXU_DIM = 256
VMEM_LIMIT = 56 * 1024 * 1024

N_LEVELS = 6


def _cparams(*sem, flags=None):
    return pltpu.CompilerParams(dimension_semantics=sem, vmem_limit_bytes=VMEM_LIMIT,
                                flags=flags)


def _sigmoid(x):
    return 1.0 / (1.0 + jnp.exp(-x))


def _silu(x):
    return x * _sigmoid(x)


def _rms(x, g):
    return x * lax.rsqrt(jnp.mean(x * x, axis=-1, keepdims=True) + EPS) * g


def _dot_nt(a, b):
    return lax.dot_general(a, b, (((1,), (1,)), ((), ())), preferred_element_type=F32)


def _dot_tn(a, b):
    return lax.dot_general(a, b, (((0,), (0,)), ((), ())), preferred_element_type=F32)


def _dot3(m_bf16, g):
    g0 = g.astype(BF16)
    r1 = g - g0.astype(F32)
    g1 = r1.astype(BF16)
    g2 = (r1 - g1.astype(F32)).astype(BF16)
    acc = jnp.dot(m_bf16, g0, preferred_element_type=F32)
    acc = acc + jnp.dot(m_bf16, g1, preferred_element_type=F32)
    return acc + jnp.dot(m_bf16, g2, preferred_element_type=F32)


def _lower_bound(lbl_ref, direction):
    l = lbl_ref[:, direction:direction + 1, :]
    m = jnp.max(l, axis=0)
    e = jnp.exp(l - m[None])
    return e[0] / jnp.sum(e, axis=0)


def _mod_kernel(cv_ref, w_ref, b_ref, o_ref):
    s = _silu(cv_ref[...])
    o_ref[...] = jnp.dot(s, w_ref[...], precision=HIGHEST,
                         preferred_element_type=F32) + b_ref[...]


def _modulation(cvec, w, b):
    rows, d = cvec.shape
    cols = w.shape[1]
    return pl.pallas_call(
        _mod_kernel,
        grid=(cols // d,),
        in_specs=[pl.BlockSpec((rows, d), lambda j: (0, 0)),
                  pl.BlockSpec((d, d), lambda j: (0, j)),
                  pl.BlockSpec((1, d), lambda j: (0, j))],
        out_specs=pl.BlockSpec((rows, d), lambda j: (0, j)),
        out_shape=jax.ShapeDtypeStruct((rows, cols), F32),
        compiler_params=_cparams("parallel"),
        name="mod",
    )(cvec, w, b.reshape(1, cols))


def _ctx_kernel(ctx_ref, mod_ref, g_ref, w_ref, lbl_ref, tri_ref, o_ref):
    x = ctx_ref[0]
    h = _rms(x, g_ref[...]) * (1.0 + mod_ref[0, 1:2, :]) + mod_ref[0, 0:1, :]
    p = jnp.dot(h.astype(BF16), w_ref[...], preferred_element_type=F32)
    v = p[:, :HGRN_WIDTH].astype(BF16)
    for direction in range(2):
        z = p[:, (1 + direction) * HGRN_WIDTH:(2 + direction) * HGRN_WIDTH]
        lb = _lower_bound(lbl_ref, direction)
        f = lb + (1.0 - lb) * _sigmoid(z)
        expo = _dot3(tri_ref[direction], jnp.log(f))
        kd = ((1.0 - f) * jnp.exp(expo)).astype(BF16)
        for hd in range(N_HEADS):
            sl = slice(hd * HEAD_DIM, (hd + 1) * HEAD_DIM)
            o_ref[0, direction, hd] = _dot_tn(v[:, sl], kd[:, sl])


def _ctx_states(ctx, mod3, norm_g, w_ctx, lb_logits):
    bsz, length, d = ctx.shape
    idx = np.arange(length)
    tri = np.stack([idx[None, :] > idx[:, None], idx[None, :] < idx[:, None]])
    tri = jnp.asarray(tri.astype(np.float32), dtype=BF16)
    ctx_row = mod3.shape[0] - 1
    return pl.pallas_call(
        _ctx_kernel,
        grid=(bsz,),
        in_specs=[pl.BlockSpec((1, length, d), lambda b: (b, 0, 0)),
                  pl.BlockSpec((1, N_MOD, d), lambda b: (ctx_row, 0, 0)),
                  pl.BlockSpec((1, d), lambda b: (0, 0)),
                  pl.BlockSpec(w_ctx.shape, lambda b: (0, 0)),
                  pl.BlockSpec(lb_logits.shape, lambda b: (0, 0, 0)),
                  pl.BlockSpec(tri.shape, lambda b: (0, 0, 0))],
        out_specs=pl.BlockSpec((1, 2, N_HEADS, HEAD_DIM, HEAD_DIM), lambda b: (b, 0, 0, 0, 0)),
        out_shape=jax.ShapeDtypeStruct((bsz, 2, N_HEADS, HEAD_DIM, HEAD_DIM), F32),
        compiler_params=_cparams("parallel"),
        name="ctx",
    )(ctx, mod3, norm_g, w_ctx, lb_logits, tri)


def _proj_kernel(x_ref, mod_ref, g_ref, w_ref, lbl_ref,
                 u_ref, q_ref, v_ref, lff_ref, lfb_ref, gs_ref):
    x = x_ref[0]
    h = _rms(x, g_ref[...]) * (1.0 + mod_ref[0, 1:2, :]) + mod_ref[0, 0:1, :]
    hb = h.astype(BF16)

    def cols(off, width):
        return jnp.dot(hb, w_ref[:, off:off + width], preferred_element_type=F32)

    u_ref[0] = (cols(0, CONV_WIDTH) * _sigmoid(cols(OFF_CONV_B, CONV_WIDTH))).astype(BF16)
    for direction, (off, out) in enumerate(((OFF_FF, lff_ref), (OFF_FB, lfb_ref))):
        lb = _lower_bound(lbl_ref, direction)
        out[0] = jnp.log2(lb + (1.0 - lb) * _sigmoid(cols(off, HGRN_WIDTH)))
    gs_ref[0] = _silu(cols(OFF_G, HGRN_WIDTH)).astype(BF16)
    q_ref[0] = cols(OFF_Q, HGRN_WIDTH).astype(BF16)
    v_ref[0] = cols(OFF_I, HGRN_WIDTH).astype(BF16)


def _project(x, mod3, norm_g, w_in, lb_logits, tn):
    bsz, n, d = x.shape
    tok = lambda b, j: (b, j, 0)
    out_block = pl.BlockSpec((1, tn, HGRN_WIDTH), tok)
    sds = lambda dt: jax.ShapeDtypeStruct((bsz, n, HGRN_WIDTH), dt)
    return pl.pallas_call(
        _proj_kernel,
        grid=(bsz, n // tn),
        in_specs=[pl.BlockSpec((1, tn, d), tok),
                  pl.BlockSpec((1, N_MOD, d), lambda b, j: (b, 0, 0)),
                  pl.BlockSpec((1, d), lambda b, j: (0, 0)),
                  pl.BlockSpec(w_in.shape, lambda b, j: (0, 0)),
                  pl.BlockSpec(lb_logits.shape, lambda b, j: (0, 0, 0))],
        out_specs=[out_block] * 6,
        out_shape=[sds(BF16), sds(BF16), sds(BF16), sds(F32), sds(F32), sds(BF16)],
        compiler_params=_cparams("parallel", "parallel"),
        name="proj",
    )(x, mod3, norm_g, w_in, lb_logits)


W_OFF = 16
W_PADDED = GRID_W + 2 * W_OFF


def _conv_kernel(u_ref, taps_ref, bias_ref, lng_ref, lnb_ref, o_ref, hbuf, vbuf):
    rows = u_ref.shape[1]
    half = CONV_WIDTH // 2
    zeros_h = jnp.zeros((rows, W_OFF, half), F32)
    hbuf[:, 0:W_OFF, :] = zeros_h
    hbuf[:, W_OFF + GRID_W:W_PADDED, :] = zeros_h
    hbuf[:, W_OFF:W_OFF + GRID_W, :] = u_ref[0, :, :, 0:half].astype(F32)
    zeros_v = jnp.zeros((W_OFF, GRID_W, half), F32)
    vbuf[0:W_OFF] = zeros_v
    vbuf[W_OFF + rows:W_OFF + rows + W_OFF] = zeros_v
    vbuf[W_OFF:W_OFF + rows] = u_ref[0, :, :, half:CONV_WIDTH].astype(F32)

    span = GRID_W + SUBLANES

    def scaled(x, k, lo):
        t = taps_ref[k, :, lo:lo + half]
        return (x.reshape(-1, SUBLANES, half) * t[None]).reshape(x.shape)

    def conv_row(r):
        acc_h = jnp.zeros((GRID_W, half), F32)
        for b in range(SUBLANES):
            part = None
            for k in range(b - 1, CONV_K, SUBLANES):
                if k < 0:
                    continue
                start = k + 1 - b
                term = scaled(hbuf[r, start:start + span, :], k, 0)
                part = term if part is None else part + term
            acc_h = acc_h + part[b:b + GRID_W]
        acc_v = jnp.zeros((GRID_W, half), F32)
        for k in range(CONV_K):
            acc_v = acc_v + scaled(vbuf[r + W_OFF + k - CONV_PAD], k, half)
        return acc_h + bias_ref[:, 0:half], acc_v + bias_ref[:, half:CONV_WIDTH]

    def norm_row(r, acc_h, acc_v):
        mu = (jnp.sum(acc_h, axis=-1, keepdims=True)
              + jnp.sum(acc_v, axis=-1, keepdims=True)) / CONV_WIDTH
        ch = acc_h - mu
        cv = acc_v - mu
        var = (jnp.sum(ch * ch, axis=-1, keepdims=True)
               + jnp.sum(cv * cv, axis=-1, keepdims=True)) / CONV_WIDTH
        rs = lax.rsqrt(var + EPS)
        yh = ch * rs * lng_ref[:, 0:half] + lnb_ref[:, 0:half]
        yv = cv * rs * lng_ref[:, half:CONV_WIDTH] + lnb_ref[:, half:CONV_WIDTH]
        o_ref[0, r, :, 0:half] = _silu(yh).astype(BF16)
        o_ref[0, r, :, half:CONV_WIDTH] = _silu(yv).astype(BF16)

    def row_pair(i, carry):
        first = conv_row(2 * i)
        second = conv_row(2 * i + 1)
        norm_row(2 * i, *first)
        norm_row(2 * i + 1, *second)
        return carry

    lax.fori_loop(0, rows // 2, row_pair, 0)


def _conv_module(u, taps, bias, ln_g, ln_b):
    bsz, n, cw = u.shape
    rows = n // GRID_W
    half = cw // 2
    assert rows % 2 == 0
    u4 = u.reshape(bsz, rows, GRID_W, cw)
    vec = lambda a: a.reshape(1, cw)
    taps = jnp.broadcast_to(taps[:, None, :], (CONV_K, SUBLANES, cw))
    out = pl.pallas_call(
        _conv_kernel,
        grid=(bsz,),
        in_specs=[pl.BlockSpec((1, rows, GRID_W, cw), lambda b: (b, 0, 0, 0)),
                  pl.BlockSpec((CONV_K, SUBLANES, cw), lambda b: (0, 0, 0)),
                  pl.BlockSpec((1, cw), lambda b: (0, 0)),
                  pl.BlockSpec((1, cw), lambda b: (0, 0)),
                  pl.BlockSpec((1, cw), lambda b: (0, 0))],
        out_specs=pl.BlockSpec((1, rows, GRID_W, cw), lambda b: (b, 0, 0, 0)),
        out_shape=jax.ShapeDtypeStruct((bsz, rows, GRID_W, cw), BF16),
        scratch_shapes=[pltpu.VMEM((rows, W_PADDED, half), F32),
                        pltpu.VMEM((rows + 2 * W_OFF, GRID_W, half), F32)],
        compiler_params=_cparams("parallel"),
        name="conv",
    )(u4, taps, vec(bias), vec(ln_g), vec(ln_b))
    return out.reshape(bsz, n, cw)


def _scan_constants():
    t = np.arange(CHUNK)
    cum_f = (t[None, :] <= t[:, None]).astype(np.float32)
    signs, masks = [], []
    for level in range(N_LEVELS):
        s = CHUNK >> (level + 1)
        upper = (t // s) % 2 == 1
        same = (t[:, None] // (2 * s)) == (t[None, :] // (2 * s))
        masks.append(same & upper[:, None] & ~upper[None, :])
        signs.append(np.where(upper, 1.0, -1.0))
    masks.append(t[:, None] == t[None, :])
    sign_f = np.broadcast_to(np.stack(signs[:-1])[:, :, None], (N_LEVELS - 1, CHUNK, HEAD_DIM))
    pair_f = np.stack(masks).astype(np.float32)
    cum = np.stack([np.tile(m, (1, 3)) for m in (cum_f, cum_f[::-1, ::-1])])
    sign = np.stack([sign_f, sign_f[:, ::-1]])
    pair = np.stack([pair_f, pair_f[:, ::-1, ::-1]])
    return (jnp.asarray(cum, dtype=BF16), jnp.asarray(sign, dtype=F32),
            jnp.asarray(pair, dtype=F32))


def _level_factors(x, f, sign_ref, direction):
    dk = x.shape[1]
    factors = []
    for level in range(N_LEVELS - 1):
        s = CHUNK >> (level + 1)
        refs = [jnp.broadcast_to(x[r:r + 1, :], (2 * s, dk))
                for r in range(s - 1 + direction, CHUNK, 2 * s)]
        x_ref = refs[0] if len(refs) == 1 else jnp.concatenate(refs, axis=0)
        factors.append(jnp.exp2((x - x_ref) * sign_ref[direction, level]))
    factors.append(f)
    return factors


def _scan_kernel(q_ref, v_ref, lff_ref, lfb_ref, gs_ref, s0_ref, ng_ref, m_ref, sg_ref, pm_ref,
                 o_ref, acc_ref, qe_ref, inc_ref, dec_ref, st_ref, *, unroll):
    n = q_ref.shape[1]
    n_chunks = n // CHUNK
    lf_refs = (lff_ref, lfb_ref)
    total_rows = (CHUNK - 1, 0)

    def local(i, carry):
        chunks = [i * unroll + u for u in range(unroll)]
        rows = [pl.ds(pl.multiple_of(c * CHUNK, CHUNK), CHUNK) for c in chunks]
        chains = [(u, d) for u in range(unroll) for d in range(2)]
        q = [q_ref[0, r, :].astype(F32) for r in rows]
        qb = [t.astype(BF16) for t in q]
        v = [v_ref[0, r, :] for r in rows]
        g, x = {}, {}
        for u, d in chains:
            g[u, d] = lf_refs[d][0, rows[u], :]
            g0 = g[u, d].astype(BF16)
            r1 = g[u, d] - g0.astype(F32)
            g1 = r1.astype(BF16)
            g2 = (r1 - g1.astype(F32)).astype(BF16)
            x[u, d] = jnp.dot(m_ref[d], jnp.concatenate([g0, g1, g2], axis=0),
                              preferred_element_type=F32)
        kb, ql, kl, kd = {}, {}, {}, {}
        for u, d in chains:
            xc = x[u, d]
            x_tot = xc[total_rows[d]:total_rows[d] + 1, :]
            f = jnp.exp2(g[u, d])
            k = 1.0 - f
            kb[u, d] = k.astype(BF16)
            qe_ref[rows[u], d * HEAD_DIM:(d + 1) * HEAD_DIM] = (q[u] * jnp.exp2(xc)).astype(BF16)
            factors = _level_factors(xc, f, sg_ref, d)
            ql[u, d] = [(q[u] * e).astype(BF16) for e in factors]
            kl[u, d] = [(k * e).astype(BF16) for e in factors[:-1]] + [kb[u, d]]
            kd[u, d] = (k * jnp.exp2(x_tot - xc)).astype(BF16)
            dec_ref[d, chunks[u]] = jnp.broadcast_to(jnp.exp2(x_tot), (SUBLANES, HEAD_DIM))
        scores = {}
        for u, d in chains:
            s_d = _dot_nt(qb[u], kb[u, d])
            scores[u, d] = jnp.where(pm_ref[d, N_LEVELS] > 0.5, s_d, 0.0)
        for level in range(N_LEVELS):
            for u, d in chains:
                s_l = _dot_nt(ql[u, d][level], kl[u, d][level])
                scores[u, d] = jnp.where(pm_ref[d, level] > 0.5, s_l, scores[u, d])
        for u in range(unroll):
            out = jnp.dot(scores[u, 0].astype(BF16), v[u], preferred_element_type=F32)
            out = out + jnp.dot(scores[u, 1].astype(BF16), v[u], preferred_element_type=F32)
            acc_ref[rows[u], :] = out
        for u, d in chains:
            inc_ref[d, chunks[u]] = _dot_tn(v[u], kd[u, d])
        return carry

    lax.fori_loop(0, n_chunks // unroll, local, 0)

    def recur(i, states):
        nxt = []
        for direction, c in ((0, i), (1, n_chunks - 1 - i)):
            st = states[direction]
            st_ref[c, :, direction * HEAD_DIM:(direction + 1) * HEAD_DIM] = st.astype(BF16)
            nxt.append(st * dec_ref[direction, c, 0:1, :] + inc_ref[direction, c])
        return tuple(nxt)

    lax.fori_loop(0, n_chunks, recur, (s0_ref[0, 0, 0], s0_ref[0, 1, 0]))

    wide = 8

    def readout(i, carry):
        chunks = [i * wide + u for u in range(wide)]
        rows = [pl.ds(pl.multiple_of(c * CHUNK, CHUNK), CHUNK) for c in chunks]
        inter = [_dot_nt(qe_ref[rows[u], :], st_ref[chunks[u]]) for u in range(wide)]
        for u in range(wide):
            o = acc_ref[rows[u], :] + inter[u]
            y = o * lax.rsqrt(jnp.mean(o * o, axis=-1, keepdims=True) + EPS)
            o_ref[0, rows[u], :] = (y * ng_ref[...]
                                    * gs_ref[0, rows[u], :].astype(F32)).astype(BF16)
        return carry

    lax.fori_loop(0, n_chunks // wide, readout, 0)


def _hgrn_scan(q, v, lff, lfb, gs, states, norm_g):
    bsz, n, width = q.shape
    n_chunks = n // CHUNK
    m_const, sg_const, pm_const = _scan_constants()
    head = pl.BlockSpec((1, n, HEAD_DIM), lambda b, h: (b, 0, h))
    return pl.pallas_call(
        functools.partial(_scan_kernel, unroll=8),
        grid=(bsz, N_HEADS),
        in_specs=[head, head, head, head, head,
                  pl.BlockSpec((1, 2, 1, HEAD_DIM, HEAD_DIM), lambda b, h: (b, 0, h, 0, 0)),
                  pl.BlockSpec((1, HEAD_DIM), lambda b, h: (0, h)),
                  pl.BlockSpec(m_const.shape, lambda b, h: (0, 0, 0)),
                  pl.BlockSpec(sg_const.shape, lambda b, h: (0, 0, 0, 0)),
                  pl.BlockSpec(pm_const.shape, lambda b, h: (0, 0, 0, 0))],
        out_specs=head,
        out_shape=jax.ShapeDtypeStruct((bsz, n, width), BF16),
        scratch_shapes=[pltpu.VMEM((n, HEAD_DIM), F32),
                        pltpu.VMEM((n, 2 * HEAD_DIM), BF16),
                        pltpu.VMEM((2, n_chunks, HEAD_DIM, HEAD_DIM), F32),
                        pltpu.VMEM((2, n_chunks, SUBLANES, HEAD_DIM), F32),
                        pltpu.VMEM((n_chunks, HEAD_DIM, 2 * HEAD_DIM), BF16)],
        compiler_params=_cparams("parallel", "parallel"),
        name="scan",
    )(q, v, lff, lfb, gs, states, norm_g.reshape(1, width), m_const, sg_const, pm_const)


def _oproj_kernel(cv_ref, hg_ref, x_ref, mod_ref, w_ref, g2_ref, rw_ref,
                  x1_ref, h2t_ref, pr_ref):
    mix = jnp.dot(cv_ref[0], w_ref[0:CONV_WIDTH, :], preferred_element_type=F32)
    mix = mix + jnp.dot(hg_ref[0], w_ref[CONV_WIDTH:, :], preferred_element_type=F32)
    x1 = x_ref[0] + mod_ref[0, 2:3, :] * mix
    x1_ref[0] = x1
    h2 = _rms(x1, g2_ref[...]) * (1.0 + mod_ref[0, 4:5, :]) + mod_ref[0, 3:4, :]
    h2t_ref[0] = h2.T.astype(BF16)
    h_hi = h2.astype(BF16)
    h_lo = (h2 - h_hi.astype(F32)).astype(BF16)
    r_hi = rw_ref[...].astype(BF16)
    r_lo = (rw_ref[...] - r_hi.astype(F32)).astype(BF16)
    logits = _dot_nt(r_hi, h_hi) + _dot_nt(r_lo, h_hi) + _dot_nt(r_hi, h_lo)
    z = jnp.exp(logits - jnp.max(logits, axis=0, keepdims=True))
    pr_ref[0] = z / jnp.sum(z, axis=0, keepdims=True)


def _out_project(conv_out, hg, x, mod3, w_out, norm2_g, router_t, tn):
    bsz, n, d = x.shape
    tok = lambda b, j: (b, j, 0)
    return pl.pallas_call(
        _oproj_kernel,
        grid=(bsz, n // tn),
        in_specs=[pl.BlockSpec((1, tn, CONV_WIDTH), tok),
                  pl.BlockSpec((1, tn, HGRN_WIDTH), tok),
                  pl.BlockSpec((1, tn, d), tok),
                  pl.BlockSpec((1, N_MOD, d), lambda b, j: (b, 0, 0)),
                  pl.BlockSpec(w_out.shape, lambda b, j: (0, 0)),
                  pl.BlockSpec((1, d), lambda b, j: (0, 0)),
                  pl.BlockSpec(router_t.shape, lambda b, j: (0, 0))],
        out_specs=[pl.BlockSpec((1, tn, d), tok),
                   pl.BlockSpec((1, d, tn), lambda b, j: (b, 0, j)),
                   pl.BlockSpec((1, N_EXPERTS, tn), lambda b, j: (b, 0, j))],
        out_shape=[jax.ShapeDtypeStruct((bsz, n, d), F32),
                   jax.ShapeDtypeStruct((bsz, d, n), BF16),
                   jax.ShapeDtypeStruct((bsz, N_EXPERTS, n), F32)],
        compiler_params=_cparams("parallel", "parallel"),
        name="oproj",
    )(conv_out, hg, x, mod3, w_out, norm2_g, router_t)


def _prefix_count(flags, tri_ref):
    n = flags.shape[1]
    run = jnp.zeros((flags.shape[0], 1), F32)
    pieces = []
    for j in range(n // LANES):
        blk = flags[:, j * LANES:(j + 1) * LANES]
        inc = jnp.dot(blk.astype(BF16), tri_ref[...], preferred_element_type=F32)
        pieces.append(inc - blk + run)
        run = run + inc[:, LANES - 1:LANES]
    return jnp.concatenate(pieces, axis=1)


def _topk_kernel(p_ref, tri_ref, tile_ref, slot_ref, off_ref, *, cap):
    p = p_ref[0]
    n_exp = p.shape[0]
    capf = jnp.float32(cap)

    def count_ge(t):
        return jnp.sum(jnp.where(p >= t, 1.0, 0.0), axis=1, keepdims=True)

    def cond(state):
        return state[2] > 0

    def body(state):
        lo, hi, _ = state
        mid = 0.5 * (lo + hi)
        ge = count_ge(mid) >= capf
        lo_n = jnp.where(ge, mid, lo)
        hi_n = jnp.where(ge, hi, mid)
        nxt = 0.5 * (lo_n + hi_n)
        open_ = jnp.where((nxt > lo_n) & (nxt < hi_n), 1, 0)
        return lo_n, hi_n, jnp.max(open_)

    lo0 = jnp.zeros((n_exp, 1), p.dtype)
    hi0 = jnp.full((n_exp, 1), 2.0, p.dtype)
    thr, _, _ = lax.while_loop(cond, body, (lo0, hi0, jnp.int32(1)))
    gt = jnp.where(p > thr, 1.0, 0.0)
    eq = jnp.where(p == thr, 1.0, 0.0)
    need = capf - jnp.sum(gt, axis=1, keepdims=True)
    sel = gt + eq * jnp.where(_prefix_count(eq, tri_ref) < need, 1.0, 0.0)
    slot = _prefix_count(sel, tri_ref)
    slot_ref[0] = jnp.where(sel > 0.5, slot, -1.0).astype(jnp.int32)
    off_ref[0] = jnp.dot(sel.astype(BF16), tile_ref[...],
                         preferred_element_type=F32).astype(jnp.int32)


def _expert_choice(probs, cap, tt):
    bsz, n_exp, n = probs.shape
    idx = np.arange(LANES)
    tri = jnp.asarray((idx[:, None] <= idx[None, :]).astype(np.float32), dtype=BF16)
    assert n // tt < LANES
    tile = jnp.asarray((np.arange(n)[:, None] < idx[None, :] * tt).astype(np.float32), dtype=BF16)
    return pl.pallas_call(
        functools.partial(_topk_kernel, cap=cap),
        grid=(bsz,),
        in_specs=[pl.BlockSpec((1, n_exp, n), lambda b: (b, 0, 0)),
                  pl.BlockSpec((LANES, LANES), lambda b: (0, 0)),
                  pl.BlockSpec((n, LANES), lambda b: (0, 0))],
        out_specs=[pl.BlockSpec((1, n_exp, n), lambda b: (b, 0, 0)),
                   pl.BlockSpec((1, n_exp, LANES), lambda b: (b, 0, 0))],
        out_shape=[jax.ShapeDtypeStruct((bsz, n_exp, n), jnp.int32),
                   jax.ShapeDtypeStruct((bsz, n_exp, LANES), jnp.int32)],
        compiler_params=_cparams("parallel"),
        name="topk",
    )(probs, tri, tile)


def _one_hot_rows(slot_row, first, rows):
    ids = lax.broadcasted_iota(jnp.int32, (rows, slot_row.shape[1]), 0) + first
    return jnp.where(ids == slot_row, 1.0, 0.0).astype(BF16)


def _pair_table_scratch(n, tt, cap, ts):
    n_pairs = n // tt + cap // ts - 1
    return [pltpu.SMEM((n_pairs,), jnp.int32), pltpu.SMEM((n_pairs,), jnp.int32)]


def _band_pairs(off_ref, slot_ref, pj_ref, pi_ref, n_tt, tt, ts, cap):
    n_st = cap // ts
    n_pairs = n_tt + n_st - 1
    base = (pl.program_id(0) * pl.num_programs(1) + pl.program_id(1)) * (n_tt + 1)
    for p in range(n_pairs):
        pj_ref[p] = 0
        pi_ref[p] = n_st
    count = jnp.int32(0)
    for i in range(n_st):
        for j in range(n_tt):
            lo = off_ref[base + j]
            hi = off_ref[base + j + 1]
            hit = (hi > lo) & (hi > i * ts) & (lo < (i + 1) * ts)

            @pl.when(hit)
            def _(i=i, j=j, count=count):
                pj_ref[count] = j
                pi_ref[count] = i

            count = count + hit.astype(jnp.int32)
    pairs = []
    for p in range(n_pairs):
        i = pi_ref[p]
        cols = pl.ds(pl.multiple_of(pj_ref[p] * tt, tt), tt)
        tile = pl.ds(pl.multiple_of(jnp.minimum(i, n_st - 1) * ts, ts), ts)
        pairs.append((cols, tile, _one_hot_rows(slot_ref[0, 0, :, cols], i * ts, ts)))
    return pairs


def _ffn_kernel(off_ref, ht_ref, slot_ref, wg_ref, wu_ref, wd_ref, o_ref, xs_ref,
                pj_ref, pi_ref, *, cap, tt, ts):
    xs_ref[...] = jnp.zeros_like(xs_ref)
    for cols, tile, onehot in _band_pairs(off_ref, slot_ref, pj_ref, pi_ref,
                                          ht_ref.shape[2] // tt, tt, ts, cap):
        xs_ref[:, tile] += _dot_nt(ht_ref[0, :, cols], onehot)
    xs = xs_ref[...].astype(BF16)
    gate_t = _dot_tn(wg_ref[0], xs)
    up_t = _dot_tn(wu_ref[0], xs)
    hid_t = (_silu(gate_t) * up_t).astype(BF16)
    o_ref[0, 0] = _dot_tn(wd_ref[0], hid_t).astype(BF16)


def _expert_ffn(off, h2t, slot4, w_gate, w_up, w_down, cap, tt, ts):
    bsz, d, n = h2t.shape
    n_exp, _, ff = w_gate.shape
    return pl.pallas_call(
        functools.partial(_ffn_kernel, cap=cap, tt=tt, ts=ts),
        grid_spec=pltpu.PrefetchScalarGridSpec(
            num_scalar_prefetch=1,
            grid=(bsz, n_exp),
            in_specs=[pl.BlockSpec((1, d, n), lambda b, e, off: (b, 0, 0)),
                      pl.BlockSpec((1, 1, 1, n), lambda b, e, off: (b, e, 0, 0)),
                      pl.BlockSpec((1, d, ff), lambda b, e, off: (e, 0, 0)),
                      pl.BlockSpec((1, d, ff), lambda b, e, off: (e, 0, 0)),
                      pl.BlockSpec((1, ff, d), lambda b, e, off: (e, 0, 0))],
            out_specs=pl.BlockSpec((1, 1, d, cap), lambda b, e, off: (b, e, 0, 0)),
            scratch_shapes=[pltpu.VMEM((d, cap), F32)] + _pair_table_scratch(n, tt, cap, ts)),
        out_shape=jax.ShapeDtypeStruct((bsz, n_exp, d, cap), BF16),
        compiler_params=_cparams("parallel", "arbitrary"),
        name="ffn",
    )(off, h2t, slot4, w_gate, w_up, w_down)


def _comb_kernel(off_ref, out_ref, slot_ref, p_ref, y_ref, pj_ref, pi_ref, *, cap, tt, ts):
    @pl.when(pl.program_id(1) == 0)
    def _():
        y_ref[...] = jnp.zeros_like(y_ref)

    for cols, tile, onehot in _band_pairs(off_ref, slot_ref, pj_ref, pi_ref,
                                          y_ref.shape[2] // tt, tt, ts, cap):
        part = jnp.dot(out_ref[0, 0, :, tile], onehot, preferred_element_type=F32)
        y_ref[0, :, cols] = y_ref[0, :, cols] + part * p_ref[0, 0, :, cols]


def _combine(off, out_t, slot4, probs4, n, tt, ts):
    bsz, n_exp, d, cap = out_t.shape
    row = pl.BlockSpec((1, 1, 1, n), lambda b, e, off: (b, e, 0, 0))
    return pl.pallas_call(
        functools.partial(_comb_kernel, cap=cap, tt=tt, ts=ts),
        grid_spec=pltpu.PrefetchScalarGridSpec(
            num_scalar_prefetch=1,
            grid=(bsz, n_exp),
            in_specs=[pl.BlockSpec((1, 1, d, cap), lambda b, e, off: (b, e, 0, 0)), row, row],
            out_specs=pl.BlockSpec((1, d, n), lambda b, e, off: (b, 0, 0)),
            scratch_shapes=_pair_table_scratch(n, tt, cap, ts)),
        out_shape=jax.ShapeDtypeStruct((bsz, d, n), F32),
        compiler_params=_cparams("parallel", "arbitrary"),
        name="comb",
    )(off, out_t, slot4, probs4)


def _final_kernel(x1_ref, yt_ref, mod_ref, g_ref, o_ref):
    x2 = x1_ref[0] + mod_ref[0, 5:6, :] * yt_ref[0].T
    o_ref[0] = _rms(x2, g_ref[...])


def _final(x1, y_t, mod3, final_g, tn):
    bsz, n, d = x1.shape
    tok = lambda b, j: (b, j, 0)
    return pl.pallas_call(
        _final_kernel,
        grid=(bsz, n // tn),
        in_specs=[pl.BlockSpec((1, tn, d), tok),
                  pl.BlockSpec((1, d, tn), lambda b, j: (b, 0, j)),
                  pl.BlockSpec((1, N_MOD, d), lambda b, j: (b, 0, 0)),
                  pl.BlockSpec((1, d), lambda b, j: (0, 0))],
        out_specs=pl.BlockSpec((1, tn, d), tok),
        out_shape=jax.ShapeDtypeStruct((bsz, n, d), F32),
        compiler_params=_cparams("parallel", "parallel"),
        name="final",
    )(x1, y_t, mod3, final_g)


def kernel(x, c, ctx, c_ctx, ada_w, ada_b, norm1_g, w_in, conv_taps, conv_bias, conv_ln_g,
           conv_ln_b, hgrn_lb_logits, hgrn_norm_g, w_out, norm2_g, router_w, w_gate, w_up,
           w_down, final_g):
    bsz, n, d = x.shape
    assert ada_w.shape[0] == 1, "single-layer stack"
    assert n % GRID_W == 0 and n % CHUNK == 0 and ctx.shape[1] % SUBLANES == 0
    cap = CAPACITY_FACTOR * n // N_EXPERTS
    tn = min(512, n)

    rows = -(-(bsz + 1) // SUBLANES) * SUBLANES
    cvec = jnp.zeros((rows, d), F32).at[:bsz].set(c).at[rows - 1].set(c_ctx)
    mod3 = _modulation(cvec, ada_w[0], ada_b[0]).reshape(rows, N_MOD, d)

    w_in_b = w_in[0].astype(BF16)
    g1 = norm1_g[0].reshape(1, d)
    states = _ctx_states(ctx, mod3, g1, w_in_b[:, OFF_I:OFF_G], hgrn_lb_logits)

    u, q, v, lff, lfb, gs = _project(x, mod3, g1, w_in_b, hgrn_lb_logits, min(2 * tn, n))
    conv_out = _conv_module(u, conv_taps[0], conv_bias[0], conv_ln_g[0], conv_ln_b[0])
    hg = _hgrn_scan(q, v, lff, lfb, gs, states, hgrn_norm_g[0])

    x1, h2t, probs = _out_project(conv_out, hg, x, mod3, w_out[0].astype(BF16),
                                  norm2_g[0].reshape(1, d), router_w[0].T, tn)
    tt = min(2 * MXU_DIM, n)
    ts = min(MXU_DIM, cap)
    slot, off = _expert_choice(probs, cap, tt)
    off = off[:, :, :n // tt + 1].reshape(-1)
    slot4 = slot.reshape(bsz, N_EXPERTS, 1, n)
    probs4 = probs.reshape(bsz, N_EXPERTS, 1, n)
    out_t = _expert_ffn(off, h2t, slot4, w_gate[0].astype(BF16), w_up[0].astype(BF16),
                        w_down[0].astype(BF16), cap, tt, ts)
    y_t = _combine(off, out_t, slot4, probs4, n, tt, ts)
    return _final(x1, y_t, mod3, final_g.reshape(1, d), tn)
```

```python
import functools

import jax
import jax.numpy as jnp
import numpy as np
from jax import lax
from jax.experimental import pallas as pl
from jax.experimental.pallas import tpu as pltpu

F32 = jnp.float32
BF16 = jnp.bfloat16
HIGHEST = lax.Precision.HIGHEST

CONV_WIDTH = 512
HGRN_WIDTH = 512
HEAD_DIM = 128
N_HEADS = HGRN_WIDTH // HEAD_DIM
CONV_K = 31
CONV_PAD = (CONV_K - 1) // 2
GRID_W = 64
CHUNK = 64
N_EXPERTS = 16
CAPACITY_FACTOR = 2
N_MOD = 6
EPS = 1e-6
OFF_CONV_B = CONV_WIDTH
OFF_Q = 2 * CONV_WIDTH
OFF_I = OFF_Q + HGRN_WIDTH
OFF_FF = OFF_I + HGRN_WIDTH
OFF_FB = OFF_FF + HGRN_WIDTH
OFF_G = OFF_FB + HGRN_WIDTH
IN_COLS = OFF_G + HGRN_WIDTH

SUBLANES = 8
LANES = 128
MXU_DIM = 256
VMEM_LIMIT = 56 * 1024 * 1024

N_LEVELS = 6


def _cparams(*sem, flags=None):
    return pltpu.CompilerParams(dimension_semantics=sem, vmem_limit_bytes=VMEM_LIMIT,
                                flags=flags)


def _sigmoid(x):
    return 1.0 / (1.0 + jnp.exp(-x))


def _silu(x):
    return x * _sigmoid(x)


def _rms(x, g):
    return x * lax.rsqrt(jnp.mean(x * x, axis=-1, keepdims=True) + EPS) * g


def _dot_nt(a, b):
    return lax.dot_general(a, b, (((1,), (1,)), ((), ())), preferred_element_type=F32)


def _dot_tn(a, b):
    return lax.dot_general(a, b, (((0,), (0,)), ((), ())), preferred_element_type=F32)


def _dot3(m_bf16, g):
    g0 = g.astype(BF16)
    r1 = g - g0.astype(F32)
    g1 = r1.astype(BF16)
    g2 = (r1 - g1.astype(F32)).astype(BF16)
    acc = jnp.dot(m_bf16, g0, preferred_element_type=F32)
    acc = acc + jnp.dot(m_bf16, g1, preferred_element_type=F32)
    return acc + jnp.dot(m_bf16, g2, preferred_element_type=F32)


def _lower_bound(lbl_ref, direction):
    l = lbl_ref[:, direction:direction + 1, :]
    m = jnp.max(l, axis=0)
    e = jnp.exp(l - m[None])
    return e[0] / jnp.sum(e, axis=0)


def _mod_kernel(cv_ref, w_ref, b_ref, o_ref):
    s = _silu(cv_ref[...])
    o_ref[...] = jnp.dot(s, w_ref[...], precision=HIGHEST,
                         preferred_element_type=F32) + b_ref[...]


def _modulation(cvec, w, b):
    rows, d = cvec.shape
    cols = w.shape[1]
    return pl.pallas_call(
        _mod_kernel,
        grid=(cols // d,),
        in_specs=[pl.BlockSpec((rows, d), lambda j: (0, 0)),
                  pl.BlockSpec((d, d), lambda j: (0, j)),
                  pl.BlockSpec((1, d), lambda j: (0, j))],
        out_specs=pl.BlockSpec((rows, d), lambda j: (0, j)),
        out_shape=jax.ShapeDtypeStruct((rows, cols), F32),
        compiler_params=_cparams("parallel"),
        name="mod",
    )(cvec, w, b.reshape(1, cols))


def _ctx_kernel(ctx_ref, mod_ref, g_ref, w_ref, lbl_ref, tri_ref, o_ref):
    x = ctx_ref[0]
    h = _rms(x, g_ref[...]) * (1.0 + mod_ref[0, 1:2, :]) + mod_ref[0, 0:1, :]
    p = jnp.dot(h.astype(BF16), w_ref[...], preferred_element_type=F32)
    v = p[:, :HGRN_WIDTH].astype(BF16)
    for direction in range(2):
        z = p[:, (1 + direction) * HGRN_WIDTH:(2 + direction) * HGRN_WIDTH]
        lb = _lower_bound(lbl_ref, direction)
        f = lb + (1.0 - lb) * _sigmoid(z)
        expo = _dot3(tri_ref[direction], jnp.log(f))
        kd = ((1.0 - f) * jnp.exp(expo)).astype(BF16)
        for hd in range(N_HEADS):
            sl = slice(hd * HEAD_DIM, (hd + 1) * HEAD_DIM)
            o_ref[0, direction, hd] = _dot_tn(v[:, sl], kd[:, sl])


def _ctx_states(ctx, mod3, norm_g, w_ctx, lb_logits):
    bsz, length, d = ctx.shape
    idx = np.arange(length)
    tri = np.stack([idx[None, :] > idx[:, None], idx[None, :] < idx[:, None]])
    tri = jnp.asarray(tri.astype(np.float32), dtype=BF16)
    ctx_row = mod3.shape[0] - 1
    return pl.pallas_call(
        _ctx_kernel,
        grid=(bsz,),
        in_specs=[pl.BlockSpec((1, length, d), lambda b: (b, 0, 0)),
                  pl.BlockSpec((1, N_MOD, d), lambda b: (ctx_row, 0, 0)),
                  pl.BlockSpec((1, d), lambda b: (0, 0)),
                  pl.BlockSpec(w_ctx.shape, lambda b: (0, 0)),
                  pl.BlockSpec(lb_logits.shape, lambda b: (0, 0, 0)),
                  pl.BlockSpec(tri.shape, lambda b: (0, 0, 0))],
        out_specs=pl.BlockSpec((1, 2, N_HEADS, HEAD_DIM, HEAD_DIM), lambda b: (b, 0, 0, 0, 0)),
        out_shape=jax.ShapeDtypeStruct((bsz, 2, N_HEADS, HEAD_DIM, HEAD_DIM), F32),
        compiler_params=_cparams("parallel"),
        name="ctx",
    )(ctx, mod3, norm_g, w_ctx, lb_logits, tri)


def _proj_kernel(x_ref, mod_ref, g_ref, w_ref, lbl_ref,
                 u_ref, q_ref, v_ref, lff_ref, lfb_ref, gs_ref):
    x = x_ref[0]
    h = _rms(x, g_ref[...]) * (1.0 + mod_ref[0, 1:2, :]) + mod_ref[0, 0:1, :]
    hb = h.astype(BF16)

    def cols(off, width):
        return jnp.dot(hb, w_ref[:, off:off + width], preferred_element_type=F32)

    u_ref[0] = (cols(0, CONV_WIDTH) * _sigmoid(cols(OFF_CONV_B, CONV_WIDTH))).astype(BF16)
    for direction, (off, out) in enumerate(((OFF_FF, lff_ref), (OFF_FB, lfb_ref))):
        lb = _lower_bound(lbl_ref, direction)
        out[0] = jnp.log2(lb + (1.0 - lb) * _sigmoid(cols(off, HGRN_WIDTH)))
    gs_ref[0] = _silu(cols(OFF_G, HGRN_WIDTH)).astype(BF16)
    q_ref[0] = cols(OFF_Q, HGRN_WIDTH).astype(BF16)
    v_ref[0] = cols(OFF_I, HGRN_WIDTH).astype(BF16)


def _project(x, mod3, norm_g, w_in, lb_logits, tn):
    bsz, n, d = x.shape
    tok = lambda b, j: (b, j, 0)
    out_block = pl.BlockSpec((1, tn, HGRN_WIDTH), tok)
    sds = lambda dt: jax.ShapeDtypeStruct((bsz, n, HGRN_WIDTH), dt)
    return pl.pallas_call(
        _proj_kernel,
        grid=(bsz, n // tn),
        in_specs=[pl.BlockSpec((1, tn, d), tok),
                  pl.BlockSpec((1, N_MOD, d), lambda b, j: (b, 0, 0)),
                  pl.BlockSpec((1, d), lambda b, j: (0, 0)),
                  pl.BlockSpec(w_in.shape, lambda b, j: (0, 0)),
                  pl.BlockSpec(lb_logits.shape, lambda b, j: (0, 0, 0))],
        out_specs=[out_block] * 6,
        out_shape=[sds(BF16), sds(BF16), sds(BF16), sds(F32), sds(F32), sds(BF16)],
        compiler_params=_cparams("parallel", "parallel"),
        name="proj",
    )(x, mod3, norm_g, w_in, lb_logits)


W_OFF = 16
W_PADDED = GRID_W + 2 * W_OFF


def _conv_kernel(u_ref, taps_ref, bias_ref, lng_ref, lnb_ref, o_ref, hbuf, vbuf):
    rows = u_ref.shape[1]
    half = CONV_WIDTH // 2
    zeros_h = jnp.zeros((rows, W_OFF, half), F32)
    hbuf[:, 0:W_OFF, :] = zeros_h
    hbuf[:, W_OFF + GRID_W:W_PADDED, :] = zeros_h
    hbuf[:, W_OFF:W_OFF + GRID_W, :] = u_ref[0, :, :, 0:half].astype(F32)
    zeros_v = jnp.zeros((W_OFF, GRID_W, half), F32)
    vbuf[0:W_OFF] = zeros_v
    vbuf[W_OFF + rows:W_OFF + rows + W_OFF] = zeros_v
    vbuf[W_OFF:W_OFF + rows] = u_ref[0, :, :, half:CONV_WIDTH].astype(F32)

    span = GRID_W + SUBLANES

    def scaled(x, k, lo):
        t = taps_ref[k, :, lo:lo + half]
        return (x.reshape(-1, SUBLANES, half) * t[None]).reshape(x.shape)

    def conv_row(r):
        acc_h = jnp.zeros((GRID_W, half), F32)
        for b in range(SUBLANES):
            part = None
            for k in range(b - 1, CONV_K, SUBLANES):
                if k < 0:
                    continue
                start = k + 1 - b
                term = scaled(hbuf[r, start:start + span, :], k, 0)
                part = term if part is None else part + term
            acc_h = acc_h + part[b:b + GRID_W]
        acc_v = jnp.zeros((GRID_W, half), F32)
        for k in range(CONV_K):
            acc_v = acc_v + scaled(vbuf[r + W_OFF + k - CONV_PAD], k, half)
        return acc_h + bias_ref[:, 0:half], acc_v + bias_ref[:, half:CONV_WIDTH]

    def norm_row(r, acc_h, acc_v):
        mu = (jnp.sum(acc_h, axis=-1, keepdims=True)
              + jnp.sum(acc_v, axis=-1, keepdims=True)) / CONV_WIDTH
        ch = acc_h - mu
        cv = acc_v - mu
        var = (jnp.sum(ch * ch, axis=-1, keepdims=True)
               + jnp.sum(cv * cv, axis=-1, keepdims=True)) / CONV_WIDTH
        rs = lax.rsqrt(var + EPS)
        yh = ch * rs * lng_ref[:, 0:half] + lnb_ref[:, 0:half]
        yv = cv * rs * lng_ref[:, half:CONV_WIDTH] + lnb_ref[:, half:CONV_WIDTH]
        o_ref[0, r, :, 0:half] = _silu(yh).astype(BF16)
        o_ref[0, r, :, half:CONV_WIDTH] = _silu(yv).astype(BF16)

    def row_pair(i, carry):
        first = conv_row(2 * i)
        second = conv_row(2 * i + 1)
        norm_row(2 * i, *first)
        norm_row(2 * i + 1, *second)
        return carry

    lax.fori_loop(0, rows // 2, row_pair, 0)


def _conv_module(u, taps, bias, ln_g, ln_b):
    bsz, n, cw = u.shape
    rows = n // GRID_W
    half = cw // 2
    assert rows % 2 == 0
    u4 = u.reshape(bsz, rows, GRID_W, cw)
    vec = lambda a: a.reshape(1, cw)
    taps = jnp.broadcast_to(taps[:, None, :], (CONV_K, SUBLANES, cw))
    out = pl.pallas_call(
        _conv_kernel,
        grid=(bsz,),
        in_specs=[pl.BlockSpec((1, rows, GRID_W, cw), lambda b: (b, 0, 0, 0)),
                  pl.BlockSpec((CONV_K, SUBLANES, cw), lambda b: (0, 0, 0)),
                  pl.BlockSpec((1, cw), lambda b: (0, 0)),
                  pl.BlockSpec((1, cw), lambda b: (0, 0)),
                  pl.BlockSpec((1, cw), lambda b: (0, 0))],
        out_specs=pl.BlockSpec((1, rows, GRID_W, cw), lambda b: (b, 0, 0, 0)),
        out_shape=jax.ShapeDtypeStruct((bsz, rows, GRID_W, cw), BF16),
        scratch_shapes=[pltpu.VMEM((rows, W_PADDED, half), F32),
                        pltpu.VMEM((rows + 2 * W_OFF, GRID_W, half), F32)],
        compiler_params=_cparams("parallel"),
        name="conv",
    )(u4, taps, vec(bias), vec(ln_g), vec(ln_b))
    return out.reshape(bsz, n, cw)


def _scan_constants():
    t = np.arange(CHUNK)
    cum_f = (t[None, :] <= t[:, None]).astype(np.float32)
    signs, masks = [], []
    for level in range(N_LEVELS):
        s = CHUNK >> (level + 1)
        upper = (t // s) % 2 == 1
        same = (t[:, None] // (2 * s)) == (t[None, :] // (2 * s))
        masks.append(same & upper[:, None] & ~upper[None, :])
        signs.append(np.where(upper, 1.0, -1.0))
    masks.append(t[:, None] == t[None, :])
    sign_f = np.broadcast_to(np.stack(signs[:-1])[:, :, None], (N_LEVELS - 1, CHUNK, HEAD_DIM))
    pair_f = np.stack(masks).astype(np.float32)
    cum = np.stack([np.tile(m, (1, 3)) for m in (cum_f, cum_f[::-1, ::-1])])
    sign = np.stack([sign_f, sign_f[:, ::-1]])
    pair = np.stack([pair_f, pair_f[:, ::-1, ::-1]])
    return (jnp.asarray(cum, dtype=BF16), jnp.asarray(sign, dtype=F32),
            jnp.asarray(pair, dtype=F32))


def _level_factors(x, f, sign_ref, direction):
    dk = x.shape[1]
    factors = []
    for level in range(N_LEVELS - 1):
        s = CHUNK >> (level + 1)
        refs = [jnp.broadcast_to(x[r:r + 1, :], (2 * s, dk))
                for r in range(s - 1 + direction, CHUNK, 2 * s)]
        x_ref = refs[0] if len(refs) == 1 else jnp.concatenate(refs, axis=0)
        factors.append(jnp.exp2((x - x_ref) * sign_ref[direction, level]))
    factors.append(f)
    return factors


def _scan_kernel(q_ref, v_ref, lff_ref, lfb_ref, gs_ref, s0_ref, ng_ref, m_ref, sg_ref, pm_ref,
                 o_ref, acc_ref, qe_ref, inc_ref, dec_ref, st_ref, *, unroll):
    n = q_ref.shape[1]
    n_chunks = n // CHUNK
    lf_refs = (lff_ref, lfb_ref)
    total_rows = (CHUNK - 1, 0)

    def local(i, carry):
        chunks = [i * unroll + u for u in range(unroll)]
        rows = [pl.ds(pl.multiple_of(c * CHUNK, CHUNK), CHUNK) for c in chunks]
        chains = [(u, d) for u in range(unroll) for d in range(2)]
        q = [q_ref[0, r, :].astype(F32) for r in rows]
        qb = [t.astype(BF16) for t in q]
        v = [v_ref[0, r, :] for r in rows]
        g, x = {}, {}
        for u, d in chains:
            g[u, d] = lf_refs[d][0, rows[u], :]
            g0 = g[u, d].astype(BF16)
            r1 = g[u, d] - g0.astype(F32)
            g1 = r1.astype(BF16)
            g2 = (r1 - g1.astype(F32)).astype(BF16)
            x[u, d] = jnp.dot(m_ref[d], jnp.concatenate([g0, g1, g2], axis=0),
                              preferred_element_type=F32)
        kb, ql, kl, kd = {}, {}, {}, {}
        for u, d in chains:
            xc = x[u, d]
            x_tot = xc[total_rows[d]:total_rows[d] + 1, :]
            f = jnp.exp2(g[u, d])
            k = 1.0 - f
            kb[u, d] = k.astype(BF16)
            qe_ref[rows[u], d * HEAD_DIM:(d + 1) * HEAD_DIM] = (q[u] * jnp.exp2(xc)).astype(BF16)
            factors = _level_factors(xc, f, sg_ref, d)
            ql[u, d] = [(q[u] * e).astype(BF16) for e in factors]
            kl[u, d] = [(k * e).astype(BF16) for e in factors[:-1]] + [kb[u, d]]
            kd[u, d] = (k * jnp.exp2(x_tot - xc)).astype(BF16)
            dec_ref[d, chunks[u]] = jnp.broadcast_to(jnp.exp2(x_tot), (SUBLANES, HEAD_DIM))
        scores = {}
        for u, d in chains:
            s_d = _dot_nt(qb[u], kb[u, d])
            scores[u, d] = jnp.where(pm_ref[d, N_LEVELS] > 0.5, s_d, 0.0)
        for level in range(N_LEVELS):
            for u, d in chains:
                s_l = _dot_nt(ql[u, d][level], kl[u, d][level])
                scores[u, d] = jnp.where(pm_ref[d, level] > 0.5, s_l, scores[u, d])
        for u in range(unroll):
            out = jnp.dot(scores[u, 0].astype(BF16), v[u], preferred_element_type=F32)
            out = out + jnp.dot(scores[u, 1].astype(BF16), v[u], preferred_element_type=F32)
            acc_ref[rows[u], :] = out
        for u, d in chains:
            inc_ref[d, chunks[u]] = _dot_tn(v[u], kd[u, d])
        return carry

    lax.fori_loop(0, n_chunks // unroll, local, 0)

    def recur(i, states):
        nxt = []
        for direction, c in ((0, i), (1, n_chunks - 1 - i)):
            st = states[direction]
            st_ref[c, :, direction * HEAD_DIM:(direction + 1) * HEAD_DIM] = st.astype(BF16)
            nxt.append(st * dec_ref[direction, c, 0:1, :] + inc_ref[direction, c])
        return tuple(nxt)

    lax.fori_loop(0, n_chunks, recur, (s0_ref[0, 0, 0], s0_ref[0, 1, 0]))

    wide = 8

    def readout(i, carry):
        chunks = [i * wide + u for u in range(wide)]
        rows = [pl.ds(pl.multiple_of(c * CHUNK, CHUNK), CHUNK) for c in chunks]
        inter = [_dot_nt(qe_ref[rows[u], :], st_ref[chunks[u]]) for u in range(wide)]
        for u in range(wide):
            o = acc_ref[rows[u], :] + inter[u]
            y = o * lax.rsqrt(jnp.mean(o * o, axis=-1, keepdims=True) + EPS)
            o_ref[0, rows[u], :] = (y * ng_ref[...]
                                    * gs_ref[0, rows[u], :].astype(F32)).astype(BF16)
        return carry

    lax.fori_loop(0, n_chunks // wide, readout, 0)


def _hgrn_scan(q, v, lff, lfb, gs, states, norm_g):
    bsz, n, width = q.shape
    n_chunks = n // CHUNK
    m_const, sg_const, pm_const = _scan_constants()
    head = pl.BlockSpec((1, n, HEAD_DIM), lambda b, h: (b, 0, h))
    return pl.pallas_call(
        functools.partial(_scan_kernel, unroll=8),
        grid=(bsz, N_HEADS),
        in_specs=[head, head, head, head, head,
                  pl.BlockSpec((1, 2, 1, HEAD_DIM, HEAD_DIM), lambda b, h: (b, 0, h, 0, 0)),
                  pl.BlockSpec((1, HEAD_DIM), lambda b, h: (0, h)),
                  pl.BlockSpec(m_const.shape, lambda b, h: (0, 0, 0)),
                  pl.BlockSpec(sg_const.shape, lambda b, h: (0, 0, 0, 0)),
                  pl.BlockSpec(pm_const.shape, lambda b, h: (0, 0, 0, 0))],
        out_specs=head,
        out_shape=jax.ShapeDtypeStruct((bsz, n, width), BF16),
        scratch_shapes=[pltpu.VMEM((n, HEAD_DIM), F32),
                        pltpu.VMEM((n, 2 * HEAD_DIM), BF16),
                        pltpu.VMEM((2, n_chunks, HEAD_DIM, HEAD_DIM), F32),
                        pltpu.VMEM((2, n_chunks, SUBLANES, HEAD_DIM), F32),
                        pltpu.VMEM((n_chunks, HEAD_DIM, 2 * HEAD_DIM), BF16)],
        compiler_params=_cparams("parallel", "parallel"),
        name="scan",
    )(q, v, lff, lfb, gs, states, norm_g.reshape(1, width), m_const, sg_const, pm_const)


def _oproj_kernel(cv_ref, hg_ref, x_ref, mod_ref, w_ref, g2_ref, rw_ref,
                  x1_ref, h2t_ref, pr_ref):
    mix = jnp.dot(cv_ref[0], w_ref[0:CONV_WIDTH, :], preferred_element_type=F32)
    mix = mix + jnp.dot(hg_ref[0], w_ref[CONV_WIDTH:, :], preferred_element_type=F32)
    x1 = x_ref[0] + mod_ref[0, 2:3, :] * mix
    x1_ref[0] = x1
    h2 = _rms(x1, g2_ref[...]) * (1.0 + mod_ref[0, 4:5, :]) + mod_ref[0, 3:4, :]
    h2t_ref[0] = h2.T.astype(BF16)
    h_hi = h2.astype(BF16)
    h_lo = (h2 - h_hi.astype(F32)).astype(BF16)
    r_hi = rw_ref[...].astype(BF16)
    r_lo = (rw_ref[...] - r_hi.astype(F32)).astype(BF16)
    logits = _dot_nt(r_hi, h_hi) + _dot_nt(r_lo, h_hi) + _dot_nt(r_hi, h_lo)
    z = jnp.exp(logits - jnp.max(logits, axis=0, keepdims=True))
    pr_ref[0] = z / jnp.sum(z, axis=0, keepdims=True)


def _out_project(conv_out, hg, x, mod3, w_out, norm2_g, router_t, tn):
    bsz, n, d = x.shape
    tok = lambda b, j: (b, j, 0)
    return pl.pallas_call(
        _oproj_kernel,
        grid=(bsz, n // tn),
        in_specs=[pl.BlockSpec((1, tn, CONV_WIDTH), tok),
                  pl.BlockSpec((1, tn, HGRN_WIDTH), tok),
                  pl.BlockSpec((1, tn, d), tok),
                  pl.BlockSpec((1, N_MOD, d), lambda b, j: (b, 0, 0)),
                  pl.BlockSpec(w_out.shape, lambda b, j: (0, 0)),
                  pl.BlockSpec((1, d), lambda b, j: (0, 0)),
                  pl.BlockSpec(router_t.shape, lambda b, j: (0, 0))],
        out_specs=[pl.BlockSpec((1, tn, d), tok),
                   pl.BlockSpec((1, d, tn), lambda b, j: (b, 0, j)),
                   pl.BlockSpec((1, N_EXPERTS, tn), lambda b, j: (b, 0, j))],
        out_shape=[jax.ShapeDtypeStruct((bsz, n, d), F32),
                   jax.ShapeDtypeStruct((bsz, d, n), BF16),
                   jax.ShapeDtypeStruct((bsz, N_EXPERTS, n), F32)],
        compiler_params=_cparams("parallel", "parallel"),
        name="oproj",
    )(conv_out, hg, x, mod3, w_out, norm2_g, router_t)


def _prefix_count(flags, tri_ref):
    n = flags.shape[1]
    run = jnp.zeros((flags.shape[0], 1), F32)
    pieces = []
    for j in range(n // LANES):
        blk = flags[:, j * LANES:(j + 1) * LANES]
        inc = jnp.dot(blk.astype(BF16), tri_ref[...], preferred_element_type=F32)
        pieces.append(inc - blk + run)
        run = run + inc[:, LANES - 1:LANES]
    return jnp.concatenate(pieces, axis=1)


def _topk_kernel(p_ref, tri_ref, tile_ref, slot_ref, off_ref, *, cap):
    p = p_ref[0]
    n_exp = p.shape[0]
    capf = jnp.float32(cap)

    def count_ge(t):
        return jnp.sum(jnp.where(p >= t, 1.0, 0.0), axis=1, keepdims=True)

    def cond(state):
        return state[2] > 0

    def body(state):
        lo, hi, _ = state
        mid = 0.5 * (lo + hi)
        ge = count_ge(mid) >= capf
        lo_n = jnp.where(ge, mid, lo)
        hi_n = jnp.where(ge, hi, mid)
        nxt = 0.5 * (lo_n + hi_n)
        open_ = jnp.where((nxt > lo_n) & (nxt < hi_n), 1, 0)
        return lo_n, hi_n, jnp.max(open_)

    lo0 = jnp.zeros((n_exp, 1), p.dtype)
    hi0 = jnp.full((n_exp, 1), 2.0, p.dtype)
    thr, _, _ = lax.while_loop(cond, body, (lo0, hi0, jnp.int32(1)))
    gt = jnp.where(p > thr, 1.0, 0.0)
    eq = jnp.where(p == thr, 1.0, 0.0)
    need = capf - jnp.sum(gt, axis=1, keepdims=True)
    sel = gt + eq * jnp.where(_prefix_count(eq, tri_ref) < need, 1.0, 0.0)
    slot = _prefix_count(sel, tri_ref)
    slot_ref[0] = jnp.where(sel > 0.5, slot, -1.0).astype(jnp.int32)
    off_ref[0] = jnp.dot(sel.astype(BF16), tile_ref[...],
                         preferred_element_type=F32).astype(jnp.int32)


def _expert_choice(probs, cap, tt):
    bsz, n_exp, n = probs.shape
    idx = np.arange(LANES)
    tri = jnp.asarray((idx[:, None] <= idx[None, :]).astype(np.float32), dtype=BF16)
    assert n // tt < LANES
    tile = jnp.asarray((np.arange(n)[:, None] < idx[None, :] * tt).astype(np.float32), dtype=BF16)
    rows = bsz * n_exp
    slot, off = pl.pallas_call(
        functools.partial(_topk_kernel, cap=cap),
        grid=(1,),
        in_specs=[pl.BlockSpec((1, rows, n), lambda b: (0, 0, 0)),
                  pl.BlockSpec((LANES, LANES), lambda b: (0, 0)),
                  pl.BlockSpec((n, LANES), lambda b: (0, 0))],
        out_specs=[pl.BlockSpec((1, rows, n), lambda b: (0, 0, 0)),
                   pl.BlockSpec((1, rows, LANES), lambda b: (0, 0, 0))],
        out_shape=[jax.ShapeDtypeStruct((1, rows, n), jnp.int32),
                   jax.ShapeDtypeStruct((1, rows, LANES), jnp.int32)],
        compiler_params=_cparams("arbitrary"),
        name="topk",
    )(probs.reshape(1, rows, n), tri, tile)
    return slot.reshape(bsz, n_exp, n), off.reshape(bsz, n_exp, LANES)


def _one_hot_rows(slot_row, first, rows):
    ids = lax.broadcasted_iota(jnp.int32, (rows, slot_row.shape[1]), 0) + first
    return jnp.where(ids == slot_row, 1.0, 0.0).astype(BF16)


def _pair_table_scratch(n, tt, cap, ts):
    n_pairs = n // tt + cap // ts - 1
    return [pltpu.SMEM((n_pairs,), jnp.int32), pltpu.SMEM((n_pairs,), jnp.int32)]


def _band_pairs(off_ref, slot_ref, pj_ref, pi_ref, n_tt, tt, ts, cap):
    n_st = cap // ts
    n_pairs = n_tt + n_st - 1
    base = (pl.program_id(0) * pl.num_programs(1) + pl.program_id(1)) * (n_tt + 1)
    for p in range(n_pairs):
        pj_ref[p] = 0
        pi_ref[p] = n_st
    count = jnp.int32(0)
    for i in range(n_st):
        for j in range(n_tt):
            lo = off_ref[base + j]
            hi = off_ref[base + j + 1]
            hit = (hi > lo) & (hi > i * ts) & (lo < (i + 1) * ts)

            @pl.when(hit)
            def _(i=i, j=j, count=count):
                pj_ref[count] = j
                pi_ref[count] = i

            count = count + hit.astype(jnp.int32)
    pairs = []
    for p in range(n_pairs):
        i = pi_ref[p]
        cols = pl.ds(pl.multiple_of(pj_ref[p] * tt, tt), tt)
        tile = pl.ds(pl.multiple_of(jnp.minimum(i, n_st - 1) * ts, ts), ts)
        pairs.append((cols, tile, _one_hot_rows(slot_ref[0, 0, :, cols], i * ts, ts)))
    return pairs


def _ffn_kernel(off_ref, ht_ref, slot_ref, wg_ref, wu_ref, wd_ref, o_ref, xs_ref,
                pj_ref, pi_ref, *, cap, tt, ts):
    xs_ref[...] = jnp.zeros_like(xs_ref)
    for cols, tile, onehot in _band_pairs(off_ref, slot_ref, pj_ref, pi_ref,
                                          ht_ref.shape[2] // tt, tt, ts, cap):
        xs_ref[:, tile] += _dot_nt(ht_ref[0, :, cols], onehot)
    xs = xs_ref[...].astype(BF16)
    gate_t = _dot_tn(wg_ref[0], xs)
    up_t = _dot_tn(wu_ref[0], xs)
    hid_t = (_silu(gate_t) * up_t).astype(BF16)
    o_ref[0, 0] = _dot_tn(wd_ref[0], hid_t).astype(BF16)


def _expert_ffn(off, h2t, slot4, w_gate, w_up, w_down, cap, tt, ts):
    bsz, d, n = h2t.shape
    n_exp, _, ff = w_gate.shape
    return pl.pallas_call(
        functools.partial(_ffn_kernel, cap=cap, tt=tt, ts=ts),
        grid_spec=pltpu.PrefetchScalarGridSpec(
            num_scalar_prefetch=1,
            grid=(bsz, n_exp),
            in_specs=[pl.BlockSpec((1, d, n), lambda b, e, off: (b, 0, 0)),
                      pl.BlockSpec((1, 1, 1, n), lambda b, e, off: (b, e, 0, 0)),
                      pl.BlockSpec((1, d, ff), lambda b, e, off: (e, 0, 0)),
                      pl.BlockSpec((1, d, ff), lambda b, e, off: (e, 0, 0)),
                      pl.BlockSpec((1, ff, d), lambda b, e, off: (e, 0, 0))],
            out_specs=pl.BlockSpec((1, 1, d, cap), lambda b, e, off: (b, e, 0, 0)),
            scratch_shapes=[pltpu.VMEM((d, cap), F32)] + _pair_table_scratch(n, tt, cap, ts)),
        out_shape=jax.ShapeDtypeStruct((bsz, n_exp, d, cap), BF16),
        compiler_params=_cparams("parallel", "arbitrary"),
        name="ffn",
    )(off, h2t, slot4, w_gate, w_up, w_down)


def _comb_kernel(off_ref, out_ref, slot_ref, p_ref, y_ref, pj_ref, pi_ref, *, cap, tt, ts):
    @pl.when(pl.program_id(1) == 0)
    def _():
        y_ref[...] = jnp.zeros_like(y_ref)

    for cols, tile, onehot in _band_pairs(off_ref, slot_ref, pj_ref, pi_ref,
                                          y_ref.shape[2] // tt, tt, ts, cap):
        part = jnp.dot(out_ref[0, 0, :, tile], onehot, preferred_element_type=F32)
        y_ref[0, :, cols] = y_ref[0, :, cols] + part * p_ref[0, 0, :, cols]


def _combine(off, out_t, slot4, probs4, n, tt, ts):
    bsz, n_exp, d, cap = out_t.shape
    row = pl.BlockSpec((1, 1, 1, n), lambda b, e, off: (b, e, 0, 0))
    return pl.pallas_call(
        functools.partial(_comb_kernel, cap=cap, tt=tt, ts=ts),
        grid_spec=pltpu.PrefetchScalarGridSpec(
            num_scalar_prefetch=1,
            grid=(bsz, n_exp),
            in_specs=[pl.BlockSpec((1, 1, d, cap), lambda b, e, off: (b, e, 0, 0)), row, row],
            out_specs=pl.BlockSpec((1, d, n), lambda b, e, off: (b, 0, 0)),
            scratch_shapes=_pair_table_scratch(n, tt, cap, ts)),
        out_shape=jax.ShapeDtypeStruct((bsz, d, n), F32),
        compiler_params=_cparams("parallel", "arbitrary"),
        name="comb",
    )(off, out_t, slot4, probs4)


def _final_kernel(x1_ref, yt_ref, mod_ref, g_ref, o_ref):
    x2 = x1_ref[0] + mod_ref[0, 5:6, :] * yt_ref[0].T
    o_ref[0] = _rms(x2, g_ref[...])


def _final(x1, y_t, mod3, final_g, tn):
    bsz, n, d = x1.shape
    tok = lambda b, j: (b, j, 0)
    return pl.pallas_call(
        _final_kernel,
        grid=(bsz, n // tn),
        in_specs=[pl.BlockSpec((1, tn, d), tok),
                  pl.BlockSpec((1, d, tn), lambda b, j: (b, 0, j)),
                  pl.BlockSpec((1, N_MOD, d), lambda b, j: (b, 0, 0)),
                  pl.BlockSpec((1, d), lambda b, j: (0, 0))],
        out_specs=pl.BlockSpec((1, tn, d), tok),
        out_shape=jax.ShapeDtypeStruct((bsz, n, d), F32),
        compiler_params=_cparams("parallel", "parallel"),
        name="final",
    )(x1, y_t, mod3, final_g)


def kernel(x, c, ctx, c_ctx, ada_w, ada_b, norm1_g, w_in, conv_taps, conv_bias, conv_ln_g,
           conv_ln_b, hgrn_lb_logits, hgrn_norm_g, w_out, norm2_g, router_w, w_gate, w_up,
           w_down, final_g):
    bsz, n, d = x.shape
    assert ada_w.shape[0] == 1, "single-layer stack"
    assert n % GRID_W == 0 and n % CHUNK == 0 and ctx.shape[1] % SUBLANES == 0
    cap = CAPACITY_FACTOR * n // N_EXPERTS
    tn = min(512, n)

    rows = -(-(bsz + 1) // SUBLANES) * SUBLANES
    cvec = jnp.zeros((rows, d), F32).at[:bsz].set(c).at[rows - 1].set(c_ctx)
    mod3 = _modulation(cvec, ada_w[0], ada_b[0]).reshape(rows, N_MOD, d)

    w_in_b = w_in[0].astype(BF16)
    g1 = norm1_g[0].reshape(1, d)
    states = _ctx_states(ctx, mod3, g1, w_in_b[:, OFF_I:OFF_G], hgrn_lb_logits)

    u, q, v, lff, lfb, gs = _project(x, mod3, g1, w_in_b, hgrn_lb_logits, min(2 * tn, n))
    conv_out = _conv_module(u, conv_taps[0], conv_bias[0], conv_ln_g[0], conv_ln_b[0])
    hg = _hgrn_scan(q, v, lff, lfb, gs, states, hgrn_norm_g[0])

    x1, h2t, probs = _out_project(conv_out, hg, x, mod3, w_out[0].astype(BF16),
                                  norm2_g[0].reshape(1, d), router_w[0].T, tn)
    tt = min(2 * MXU_DIM, n)
    ts = min(MXU_DIM, cap)
    slot, off = _expert_choice(probs, cap, tt)
    off = off[:, :, :n // tt + 1].reshape(-1)
    slot4 = slot.reshape(bsz, N_EXPERTS, 1, n)
    probs4 = probs.reshape(bsz, N_EXPERTS, 1, n)
    out_t = _expert_ffn(off, h2t, slot4, w_gate[0].astype(BF16), w_up[0].astype(BF16),
                        w_down[0].astype(BF16), cap, tt, ts)
    y_t = _combine(off, out_t, slot4, probs4, n, tt, ts)
    return _final(x1, y_t, mod3, final_g.reshape(1, d), tn)
```

```python
import functools
import math

import jax
import jax.numpy as jnp
import numpy as np
from jax import lax
from jax.experimental import pallas as pl
from jax.experimental.pallas import tpu as pltpu

F32 = jnp.float32
BF16 = jnp.bfloat16
HIGHEST = lax.Precision.HIGHEST

CONV_WIDTH = 512
HGRN_WIDTH = 512
HEAD_DIM = 128
N_HEADS = HGRN_WIDTH // HEAD_DIM
CONV_K = 31
CONV_PAD = (CONV_K - 1) // 2
GRID_W = 64
CHUNK = 64
N_EXPERTS = 16
CAPACITY_FACTOR = 2
N_MOD = 6
EPS = 1e-6
OFF_CONV_B = CONV_WIDTH
OFF_Q = 2 * CONV_WIDTH
OFF_I = OFF_Q + HGRN_WIDTH
OFF_FF = OFF_I + HGRN_WIDTH
OFF_FB = OFF_FF + HGRN_WIDTH
OFF_G = OFF_FB + HGRN_WIDTH
IN_COLS = OFF_G + HGRN_WIDTH

SUBLANES = 8
LANES = 128
MXU_DIM = 256
VMEM_LIMIT = 56 * 1024 * 1024

N_LEVELS = 6


def _cparams(*sem, flags=None):
    return pltpu.CompilerParams(dimension_semantics=sem, vmem_limit_bytes=VMEM_LIMIT,
                                flags=flags)


def _sigmoid(x):
    return 1.0 / (1.0 + jnp.exp(-x))


def _silu(x):
    return x * _sigmoid(x)


def _rms(x, g):
    return x * lax.rsqrt(jnp.mean(x * x, axis=-1, keepdims=True) + EPS) * g


def _dot_nt(a, b):
    return lax.dot_general(a, b, (((1,), (1,)), ((), ())), preferred_element_type=F32)


def _dot_tn(a, b):
    return lax.dot_general(a, b, (((0,), (0,)), ((), ())), preferred_element_type=F32)


def _dot3(m_bf16, g):
    g0 = g.astype(BF16)
    r1 = g - g0.astype(F32)
    g1 = r1.astype(BF16)
    g2 = (r1 - g1.astype(F32)).astype(BF16)
    acc = jnp.dot(m_bf16, g0, preferred_element_type=F32)
    acc = acc + jnp.dot(m_bf16, g1, preferred_element_type=F32)
    return acc + jnp.dot(m_bf16, g2, preferred_element_type=F32)


def _lower_bound(lbl_ref, direction):
    l = lbl_ref[:, direction:direction + 1, :]
    m = jnp.max(l, axis=0)
    e = jnp.exp(l - m[None])
    return e[0] / jnp.sum(e, axis=0)


def _mod_kernel(cv_ref, w_ref, b_ref, o_ref):
    s = _silu(cv_ref[...])
    o_ref[...] = jnp.dot(s, w_ref[...], precision=HIGHEST,
                         preferred_element_type=F32) + b_ref[...]


def _modulation(cvec, w, b):
    rows, d = cvec.shape
    cols = w.shape[1]
    return pl.pallas_call(
        _mod_kernel,
        grid=(cols // d,),
        in_specs=[pl.BlockSpec((rows, d), lambda j: (0, 0)),
                  pl.BlockSpec((d, d), lambda j: (0, j)),
                  pl.BlockSpec((1, d), lambda j: (0, j))],
        out_specs=pl.BlockSpec((rows, d), lambda j: (0, j)),
        out_shape=jax.ShapeDtypeStruct((rows, cols), F32),
        compiler_params=_cparams("parallel"),
        name="mod",
    )(cvec, w, b.reshape(1, cols))


def _ctx_kernel(ctx_ref, mod_ref, g_ref, w_ref, lbl_ref, tri_ref, o_ref):
    x = ctx_ref[0]
    h = _rms(x, g_ref[...]) * (1.0 + mod_ref[0, 1:2, :]) + mod_ref[0, 0:1, :]
    p = jnp.dot(h.astype(BF16), w_ref[...], preferred_element_type=F32)
    v = p[:, :HGRN_WIDTH].astype(BF16)
    for direction in range(2):
        z = p[:, (1 + direction) * HGRN_WIDTH:(2 + direction) * HGRN_WIDTH]
        lb = _lower_bound(lbl_ref, direction)
        f = lb + (1.0 - lb) * _sigmoid(z)
        expo = _dot3(tri_ref[direction], jnp.log(f))
        kd = ((1.0 - f) * jnp.exp(expo)).astype(BF16)
        for hd in range(N_HEADS):
            sl = slice(hd * HEAD_DIM, (hd + 1) * HEAD_DIM)
            o_ref[0, direction, hd] = _dot_tn(v[:, sl], kd[:, sl])


def _ctx_states(ctx, mod3, norm_g, w_ctx, lb_logits):
    bsz, length, d = ctx.shape
    idx = np.arange(length)
    tri = np.stack([idx[None, :] > idx[:, None], idx[None, :] < idx[:, None]])
    tri = jnp.asarray(tri.astype(np.float32), dtype=BF16)
    ctx_row = mod3.shape[0] - 1
    return pl.pallas_call(
        _ctx_kernel,
        grid=(bsz,),
        in_specs=[pl.BlockSpec((1, length, d), lambda b: (b, 0, 0)),
                  pl.BlockSpec((1, N_MOD, d), lambda b: (ctx_row, 0, 0)),
                  pl.BlockSpec((1, d), lambda b: (0, 0)),
                  pl.BlockSpec(w_ctx.shape, lambda b: (0, 0)),
                  pl.BlockSpec(lb_logits.shape, lambda b: (0, 0, 0)),
                  pl.BlockSpec(tri.shape, lambda b: (0, 0, 0))],
        out_specs=pl.BlockSpec((1, 2, N_HEADS, HEAD_DIM, HEAD_DIM), lambda b: (b, 0, 0, 0, 0)),
        out_shape=jax.ShapeDtypeStruct((bsz, 2, N_HEADS, HEAD_DIM, HEAD_DIM), F32),
        compiler_params=_cparams("parallel"),
        name="ctx",
    )(ctx, mod3, norm_g, w_ctx, lb_logits, tri)


def _proj_kernel(x_ref, mod_ref, g_ref, w_ref, lbl_ref,
                 u_ref, q_ref, v_ref, lff_ref, lfb_ref, gs_ref):
    x = x_ref[0]
    h = _rms(x, g_ref[...]) * (1.0 + mod_ref[0, 1:2, :]) + mod_ref[0, 0:1, :]
    hb = h.astype(BF16)

    def cols(off, width):
        return jnp.dot(hb, w_ref[:, off:off + width], preferred_element_type=F32)

    u_ref[0] = (cols(0, CONV_WIDTH) * _sigmoid(cols(OFF_CONV_B, CONV_WIDTH))).astype(BF16)
    for direction, (off, out) in enumerate(((OFF_FF, lff_ref), (OFF_FB, lfb_ref))):
        lb = _lower_bound(lbl_ref, direction)
        out[0] = jnp.log2(lb + (1.0 - lb) * _sigmoid(cols(off, HGRN_WIDTH)))
    gs_ref[0] = _silu(cols(OFF_G, HGRN_WIDTH)).astype(BF16)
    q_ref[0] = cols(OFF_Q, HGRN_WIDTH).astype(BF16)
    v_ref[0] = cols(OFF_I, HGRN_WIDTH).astype(BF16)


def _project(x, mod3, norm_g, w_in, lb_logits, tn):
    bsz, n, d = x.shape
    tok = lambda b, j: (b, j, 0)
    out_block = pl.BlockSpec((1, tn, HGRN_WIDTH), tok)
    sds = lambda dt: jax.ShapeDtypeStruct((bsz, n, HGRN_WIDTH), dt)
    return pl.pallas_call(
        _proj_kernel,
        grid=(bsz, n // tn),
        in_specs=[pl.BlockSpec((1, tn, d), tok),
                  pl.BlockSpec((1, N_MOD, d), lambda b, j: (b, 0, 0)),
                  pl.BlockSpec((1, d), lambda b, j: (0, 0)),
                  pl.BlockSpec(w_in.shape, lambda b, j: (0, 0)),
                  pl.BlockSpec(lb_logits.shape, lambda b, j: (0, 0, 0))],
        out_specs=[out_block] * 6,
        out_shape=[sds(BF16), sds(BF16), sds(BF16), sds(F32), sds(F32), sds(BF16)],
        compiler_params=_cparams("parallel", "parallel"),
        name="proj",
    )(x, mod3, norm_g, w_in, lb_logits)


W_OFF = 16
W_PADDED = GRID_W + 2 * W_OFF


def _conv_kernel(u_ref, taps_ref, bias_ref, lng_ref, lnb_ref, o_ref, hbuf, vbuf):
    rows = u_ref.shape[1]
    half = CONV_WIDTH // 2
    zeros_h = jnp.zeros((rows, W_OFF, half), F32)
    hbuf[:, 0:W_OFF, :] = zeros_h
    hbuf[:, W_OFF + GRID_W:W_PADDED, :] = zeros_h
    hbuf[:, W_OFF:W_OFF + GRID_W, :] = u_ref[0, :, :, 0:half].astype(F32)
    zeros_v = jnp.zeros((W_OFF, GRID_W, half), F32)
    vbuf[0:W_OFF] = zeros_v
    vbuf[W_OFF + rows:W_OFF + rows + W_OFF] = zeros_v
    vbuf[W_OFF:W_OFF + rows] = u_ref[0, :, :, half:CONV_WIDTH].astype(F32)

    span = GRID_W + SUBLANES

    def scaled(x, k, lo):
        t = taps_ref[k, :, lo:lo + half]
        return (x.reshape(-1, SUBLANES, half) * t[None]).reshape(x.shape)

    def conv_row(r):
        acc_h = jnp.zeros((GRID_W, half), F32)
        for b in range(SUBLANES):
            part = None
            for k in range(b - 1, CONV_K, SUBLANES):
                if k < 0:
                    continue
                start = k + 1 - b
                term = scaled(hbuf[r, start:start + span, :], k, 0)
                part = term if part is None else part + term
            acc_h = acc_h + part[b:b + GRID_W]
        acc_v = jnp.zeros((GRID_W, half), F32)
        for k in range(CONV_K):
            acc_v = acc_v + scaled(vbuf[r + W_OFF + k - CONV_PAD], k, half)
        return acc_h + bias_ref[:, 0:half], acc_v + bias_ref[:, half:CONV_WIDTH]

    def norm_row(r, acc_h, acc_v):
        mu = (jnp.sum(acc_h, axis=-1, keepdims=True)
              + jnp.sum(acc_v, axis=-1, keepdims=True)) / CONV_WIDTH
        ch = acc_h - mu
        cv = acc_v - mu
        var = (jnp.sum(ch * ch, axis=-1, keepdims=True)
               + jnp.sum(cv * cv, axis=-1, keepdims=True)) / CONV_WIDTH
        rs = lax.rsqrt(var + EPS)
        yh = ch * rs * lng_ref[:, 0:half] + lnb_ref[:, 0:half]
        yv = cv * rs * lng_ref[:, half:CONV_WIDTH] + lnb_ref[:, half:CONV_WIDTH]
        o_ref[0, r, :, 0:half] = _silu(yh).astype(BF16)
        o_ref[0, r, :, half:CONV_WIDTH] = _silu(yv).astype(BF16)

    def row_pair(i, carry):
        first = conv_row(2 * i)
        second = conv_row(2 * i + 1)
        norm_row(2 * i, *first)
        norm_row(2 * i + 1, *second)
        return carry

    lax.fori_loop(0, rows // 2, row_pair, 0)


def _conv_module(u, taps, bias, ln_g, ln_b):
    bsz, n, cw = u.shape
    rows = n // GRID_W
    half = cw // 2
    assert rows % 2 == 0
    u4 = u.reshape(bsz, rows, GRID_W, cw)
    vec = lambda a: a.reshape(1, cw)
    taps = jnp.broadcast_to(taps[:, None, :], (CONV_K, SUBLANES, cw))
    out = pl.pallas_call(
        _conv_kernel,
        grid=(bsz,),
        in_specs=[pl.BlockSpec((1, rows, GRID_W, cw), lambda b: (b, 0, 0, 0)),
                  pl.BlockSpec((CONV_K, SUBLANES, cw), lambda b: (0, 0, 0)),
                  pl.BlockSpec((1, cw), lambda b: (0, 0)),
                  pl.BlockSpec((1, cw), lambda b: (0, 0)),
                  pl.BlockSpec((1, cw), lambda b: (0, 0))],
        out_specs=pl.BlockSpec((1, rows, GRID_W, cw), lambda b: (b, 0, 0, 0)),
        out_shape=jax.ShapeDtypeStruct((bsz, rows, GRID_W, cw), BF16),
        scratch_shapes=[pltpu.VMEM((rows, W_PADDED, half), F32),
                        pltpu.VMEM((rows + 2 * W_OFF, GRID_W, half), F32)],
        compiler_params=_cparams("parallel"),
        name="conv",
    )(u4, taps, vec(bias), vec(ln_g), vec(ln_b))
    return out.reshape(bsz, n, cw)


def _scan_constants():
    t = np.arange(CHUNK)
    cum_f = (t[None, :] <= t[:, None]).astype(np.float32)
    signs, masks = [], []
    for level in range(N_LEVELS):
        s = CHUNK >> (level + 1)
        upper = (t // s) % 2 == 1
        same = (t[:, None] // (2 * s)) == (t[None, :] // (2 * s))
        masks.append(same & upper[:, None] & ~upper[None, :])
        signs.append(np.where(upper, 1.0, -1.0))
    masks.append(t[:, None] == t[None, :])
    sign_f = np.broadcast_to(np.stack(signs[:-1])[:, :, None], (N_LEVELS - 1, CHUNK, HEAD_DIM))
    pair_f = np.stack(masks).astype(np.float32)
    cum = np.stack([np.tile(m, (1, 3)) for m in (cum_f, cum_f[::-1, ::-1])])
    sign = np.stack([sign_f, sign_f[:, ::-1]])
    pair = np.stack([pair_f, pair_f[:, ::-1, ::-1]])
    return (jnp.asarray(cum, dtype=BF16), jnp.asarray(sign, dtype=F32),
            jnp.asarray(pair, dtype=F32))


def _level_factors(x, f, sign_ref, direction):
    dk = x.shape[1]
    factors = []
    for level in range(N_LEVELS - 1):
        s = CHUNK >> (level + 1)
        refs = [jnp.broadcast_to(x[r:r + 1, :], (2 * s, dk))
                for r in range(s - 1 + direction, CHUNK, 2 * s)]
        x_ref = refs[0] if len(refs) == 1 else jnp.concatenate(refs, axis=0)
        factors.append(jnp.exp2((x - x_ref) * sign_ref[direction, level]))
    factors.append(f)
    return factors


def _scan_kernel(q_ref, v_ref, lff_ref, lfb_ref, gs_ref, s0_ref, ng_ref, m_ref, sg_ref, pm_ref,
                 o_ref, acc_ref, qe_ref, inc_ref, dec_ref, st_ref, *, unroll):
    n = q_ref.shape[1]
    n_chunks = n // CHUNK
    lf_refs = (lff_ref, lfb_ref)
    total_rows = (CHUNK - 1, 0)

    def local(i, carry):
        chunks = [i * unroll + u for u in range(unroll)]
        rows = [pl.ds(pl.multiple_of(c * CHUNK, CHUNK), CHUNK) for c in chunks]
        chains = [(u, d) for u in range(unroll) for d in range(2)]
        q = [q_ref[0, r, :].astype(F32) for r in rows]
        qb = [t.astype(BF16) for t in q]
        v = [v_ref[0, r, :] for r in rows]
        g, x = {}, {}
        for u, d in chains:
            g[u, d] = lf_refs[d][0, rows[u], :]
            g0 = g[u, d].astype(BF16)
            r1 = g[u, d] - g0.astype(F32)
            g1 = r1.astype(BF16)
            g2 = (r1 - g1.astype(F32)).astype(BF16)
            x[u, d] = jnp.dot(m_ref[d], jnp.concatenate([g0, g1, g2], axis=0),
                              preferred_element_type=F32)
        kb, ql, kl, kd = {}, {}, {}, {}
        for u, d in chains:
            xc = x[u, d]
            x_tot = xc[total_rows[d]:total_rows[d] + 1, :]
            f = jnp.exp2(g[u, d])
            k = 1.0 - f
            kb[u, d] = k.astype(BF16)
            qe_ref[rows[u], d * HEAD_DIM:(d + 1) * HEAD_DIM] = (q[u] * jnp.exp2(xc)).astype(BF16)
            factors = _level_factors(xc, f, sg_ref, d)
            ql[u, d] = [(q[u] * e).astype(BF16) for e in factors]
            kl[u, d] = [(k * e).astype(BF16) for e in factors[:-1]] + [kb[u, d]]
            kd[u, d] = (k * jnp.exp2(x_tot - xc)).astype(BF16)
            dec_ref[d, chunks[u]] = jnp.broadcast_to(jnp.exp2(x_tot), (SUBLANES, HEAD_DIM))
        scores = {}
        for u, d in chains:
            s_d = _dot_nt(qb[u], kb[u, d])
            scores[u, d] = jnp.where(pm_ref[d, N_LEVELS] > 0.5, s_d, 0.0)
        for level in range(N_LEVELS):
            for u, d in chains:
                s_l = _dot_nt(ql[u, d][level], kl[u, d][level])
                scores[u, d] = jnp.where(pm_ref[d, level] > 0.5, s_l, scores[u, d])
        for u in range(unroll):
            out = jnp.dot(scores[u, 0].astype(BF16), v[u], preferred_element_type=F32)
            out = out + jnp.dot(scores[u, 1].astype(BF16), v[u], preferred_element_type=F32)
            acc_ref[rows[u], :] = out
        for u, d in chains:
            inc_ref[d, chunks[u]] = _dot_tn(v[u], kd[u, d])
        return carry

    lax.fori_loop(0, n_chunks // unroll, local, 0)

    def recur(i, states):
        nxt = []
        for direction, c in ((0, i), (1, n_chunks - 1 - i)):
            st = states[direction]
            st_ref[c, :, direction * HEAD_DIM:(direction + 1) * HEAD_DIM] = st.astype(BF16)
            nxt.append(st * dec_ref[direction, c, 0:1, :] + inc_ref[direction, c])
        return tuple(nxt)

    lax.fori_loop(0, n_chunks, recur, (s0_ref[0, 0, 0], s0_ref[0, 1, 0]))

    wide = math.gcd(n_chunks, 32)

    def readout(i, carry):
        chunks = [i * wide + u for u in range(wide)]
        rows = [pl.ds(pl.multiple_of(c * CHUNK, CHUNK), CHUNK) for c in chunks]
        inter = [_dot_nt(qe_ref[rows[u], :], st_ref[chunks[u]]) for u in range(wide)]
        for u in range(wide):
            o = acc_ref[rows[u], :] + inter[u]
            y = o * lax.rsqrt(jnp.mean(o * o, axis=-1, keepdims=True) + EPS)
            o_ref[0, rows[u], :] = (y * ng_ref[...]
                                    * gs_ref[0, rows[u], :].astype(F32)).astype(BF16)
        return carry

    lax.fori_loop(0, n_chunks // wide, readout, 0)


def _hgrn_scan(q, v, lff, lfb, gs, states, norm_g):
    bsz, n, width = q.shape
    n_chunks = n // CHUNK
    m_const, sg_const, pm_const = _scan_constants()
    head = pl.BlockSpec((1, n, HEAD_DIM), lambda b, h: (b, 0, h))
    return pl.pallas_call(
        functools.partial(_scan_kernel, unroll=math.gcd(n_chunks, 8)),
        grid=(bsz, N_HEADS),
        in_specs=[head, head, head, head, head,
                  pl.BlockSpec((1, 2, 1, HEAD_DIM, HEAD_DIM), lambda b, h: (b, 0, h, 0, 0)),
                  pl.BlockSpec((1, HEAD_DIM), lambda b, h: (0, h)),
                  pl.BlockSpec(m_const.shape, lambda b, h: (0, 0, 0)),
                  pl.BlockSpec(sg_const.shape, lambda b, h: (0, 0, 0, 0)),
                  pl.BlockSpec(pm_const.shape, lambda b, h: (0, 0, 0, 0))],
        out_specs=head,
        out_shape=jax.ShapeDtypeStruct((bsz, n, width), BF16),
        scratch_shapes=[pltpu.VMEM((n, HEAD_DIM), F32),
                        pltpu.VMEM((n, 2 * HEAD_DIM), BF16),
                        pltpu.VMEM((2, n_chunks, HEAD_DIM, HEAD_DIM), F32),
                        pltpu.VMEM((2, n_chunks, SUBLANES, HEAD_DIM), F32),
                        pltpu.VMEM((n_chunks, HEAD_DIM, 2 * HEAD_DIM), BF16)],
        compiler_params=_cparams("parallel", "parallel"),
        name="scan",
    )(q, v, lff, lfb, gs, states, norm_g.reshape(1, width), m_const, sg_const, pm_const)


def _oproj_kernel(cv_ref, hg_ref, x_ref, mod_ref, w_ref, g2_ref, rw_ref,
                  x1_ref, h2t_ref, pr_ref):
    mix = jnp.dot(cv_ref[0], w_ref[0:CONV_WIDTH, :], preferred_element_type=F32)
    mix = mix + jnp.dot(hg_ref[0], w_ref[CONV_WIDTH:, :], preferred_element_type=F32)
    x1 = x_ref[0] + mod_ref[0, 2:3, :] * mix
    x1_ref[0] = x1
    h2 = _rms(x1, g2_ref[...]) * (1.0 + mod_ref[0, 4:5, :]) + mod_ref[0, 3:4, :]
    h2t_ref[0] = h2.T.astype(BF16)
    h_hi = h2.astype(BF16)
    h_lo = (h2 - h_hi.astype(F32)).astype(BF16)
    r_hi = rw_ref[...].astype(BF16)
    r_lo = (rw_ref[...] - r_hi.astype(F32)).astype(BF16)
    logits = _dot_nt(r_hi, h_hi) + _dot_nt(r_lo, h_hi) + _dot_nt(r_hi, h_lo)
    z = jnp.exp(logits - jnp.max(logits, axis=0, keepdims=True))
    pr_ref[0] = z / jnp.sum(z, axis=0, keepdims=True)


def _out_project(conv_out, hg, x, mod3, w_out, norm2_g, router_t, tn):
    bsz, n, d = x.shape
    tok = lambda b, j: (b, j, 0)
    return pl.pallas_call(
        _oproj_kernel,
        grid=(bsz, n // tn),
        in_specs=[pl.BlockSpec((1, tn, CONV_WIDTH), tok),
                  pl.BlockSpec((1, tn, HGRN_WIDTH), tok),
                  pl.BlockSpec((1, tn, d), tok),
                  pl.BlockSpec((1, N_MOD, d), lambda b, j: (b, 0, 0)),
                  pl.BlockSpec(w_out.shape, lambda b, j: (0, 0)),
                  pl.BlockSpec((1, d), lambda b, j: (0, 0)),
                  pl.BlockSpec(router_t.shape, lambda b, j: (0, 0))],
        out_specs=[pl.BlockSpec((1, tn, d), tok),
                   pl.BlockSpec((1, d, tn), lambda b, j: (b, 0, j)),
                   pl.BlockSpec((1, N_EXPERTS, tn), lambda b, j: (b, 0, j))],
        out_shape=[jax.ShapeDtypeStruct((bsz, n, d), F32),
                   jax.ShapeDtypeStruct((bsz, d, n), BF16),
                   jax.ShapeDtypeStruct((bsz, N_EXPERTS, n), F32)],
        compiler_params=_cparams("parallel", "parallel"),
        name="oproj",
    )(conv_out, hg, x, mod3, w_out, norm2_g, router_t)


def _prefix_count(flags, tri_ref):
    n = flags.shape[1]
    run = jnp.zeros((flags.shape[0], 1), F32)
    pieces = []
    for j in range(n // LANES):
        blk = flags[:, j * LANES:(j + 1) * LANES]
        inc = jnp.dot(blk.astype(BF16), tri_ref[...], preferred_element_type=F32)
        pieces.append(inc - blk + run)
        run = run + inc[:, LANES - 1:LANES]
    return jnp.concatenate(pieces, axis=1)


def _topk_kernel(p_ref, tri_ref, tile_ref, slot_ref, off_ref, *, cap):
    p = p_ref[0]
    n_exp = p.shape[0]
    capf = jnp.float32(cap)

    def count_ge(t):
        return jnp.sum(jnp.where(p >= t, 1.0, 0.0), axis=1, keepdims=True)

    def cond(state):
        return state[2] > 0

    def body(state):
        lo, hi, _ = state
        mid = 0.5 * (lo + hi)
        ge = count_ge(mid) >= capf
        lo_n = jnp.where(ge, mid, lo)
        hi_n = jnp.where(ge, hi, mid)
        nxt = 0.5 * (lo_n + hi_n)
        open_ = jnp.where((nxt > lo_n) & (nxt < hi_n), 1, 0)
        return lo_n, hi_n, jnp.max(open_)

    lo0 = jnp.zeros((n_exp, 1), p.dtype)
    hi0 = jnp.full((n_exp, 1), 2.0, p.dtype)
    thr, _, _ = lax.while_loop(cond, body, (lo0, hi0, jnp.int32(1)))
    gt = jnp.where(p > thr, 1.0, 0.0)
    eq = jnp.where(p == thr, 1.0, 0.0)
    need = capf - jnp.sum(gt, axis=1, keepdims=True)
    sel = gt + eq * jnp.where(_prefix_count(eq, tri_ref) < need, 1.0, 0.0)
    slot = _prefix_count(sel, tri_ref)
    slot_ref[0] = jnp.where(sel > 0.5, slot, -1.0).astype(jnp.int32)
    off_ref[0] = jnp.dot(sel.astype(BF16), tile_ref[...],
                         preferred_element_type=F32).astype(jnp.int32)


def _expert_choice(probs, cap, tt):
    bsz, n_exp, n = probs.shape
    idx = np.arange(LANES)
    tri = jnp.asarray((idx[:, None] <= idx[None, :]).astype(np.float32), dtype=BF16)
    assert n // tt < LANES
    tile = jnp.asarray((np.arange(n)[:, None] < idx[None, :] * tt).astype(np.float32), dtype=BF16)
    rows = bsz * n_exp
    slot, off = pl.pallas_call(
        functools.partial(_topk_kernel, cap=cap),
        grid=(1,),
        in_specs=[pl.BlockSpec((1, rows, n), lambda b: (0, 0, 0)),
                  pl.BlockSpec((LANES, LANES), lambda b: (0, 0)),
                  pl.BlockSpec((n, LANES), lambda b: (0, 0))],
        out_specs=[pl.BlockSpec((1, rows, n), lambda b: (0, 0, 0)),
                   pl.BlockSpec((1, rows, LANES), lambda b: (0, 0, 0))],
        out_shape=[jax.ShapeDtypeStruct((1, rows, n), jnp.int32),
                   jax.ShapeDtypeStruct((1, rows, LANES), jnp.int32)],
        compiler_params=_cparams("arbitrary"),
        name="topk",
    )(probs.reshape(1, rows, n), tri, tile)
    return slot.reshape(bsz, n_exp, n), off.reshape(bsz, n_exp, LANES)


def _one_hot_rows(slot_row, first, rows):
    ids = lax.broadcasted_iota(jnp.int32, (rows, slot_row.shape[1]), 0) + first
    return jnp.where(ids == slot_row, 1.0, 0.0).astype(BF16)


def _pair_table_scratch(n, tt, cap, ts):
    n_pairs = n // tt + cap // ts - 1
    return [pltpu.SMEM((n_pairs,), jnp.int32), pltpu.SMEM((n_pairs,), jnp.int32)]


def _band_pairs(off_ref, slot_ref, pj_ref, pi_ref, n_tt, tt, ts, cap):
    n_st = cap // ts
    n_pairs = n_tt + n_st - 1
    base = (pl.program_id(0) * pl.num_programs(1) + pl.program_id(1)) * (n_tt + 1)
    for p in range(n_pairs):
        pj_ref[p] = 0
        pi_ref[p] = n_st
    count = jnp.int32(0)
    for i in range(n_st):
        for j in range(n_tt):
            lo = off_ref[base + j]
            hi = off_ref[base + j + 1]
            hit = (hi > lo) & (hi > i * ts) & (lo < (i + 1) * ts)

            @pl.when(hit)
            def _(i=i, j=j, count=count):
                pj_ref[count] = j
                pi_ref[count] = i

            count = count + hit.astype(jnp.int32)
    pairs = []
    for p in range(n_pairs):
        i = pi_ref[p]
        cols = pl.ds(pl.multiple_of(pj_ref[p] * tt, tt), tt)
        tile = pl.ds(pl.multiple_of(jnp.minimum(i, n_st - 1) * ts, ts), ts)
        pairs.append((cols, tile, _one_hot_rows(slot_ref[0, 0, :, cols], i * ts, ts)))
    return pairs


def _ffn_kernel(off_ref, ht_ref, slot_ref, wg_ref, wu_ref, wd_ref, o_ref, xs_ref,
                pj_ref, pi_ref, *, cap, tt, ts):
    xs_ref[...] = jnp.zeros_like(xs_ref)
    for cols, tile, onehot in _band_pairs(off_ref, slot_ref, pj_ref, pi_ref,
                                          ht_ref.shape[2] // tt, tt, ts, cap):
        xs_ref[:, tile] += _dot_nt(ht_ref[0, :, cols], onehot)
    xs = xs_ref[...].astype(BF16)
    gate_t = _dot_tn(wg_ref[0], xs)
    up_t = _dot_tn(wu_ref[0], xs)
    hid_t = (_silu(gate_t) * up_t).astype(BF16)
    o_ref[0, 0] = _dot_tn(wd_ref[0], hid_t).astype(BF16)


def _expert_ffn(off, h2t, slot4, w_gate, w_up, w_down, cap, tt, ts):
    bsz, d, n = h2t.shape
    n_exp, _, ff = w_gate.shape
    return pl.pallas_call(
        functools.partial(_ffn_kernel, cap=cap, tt=tt, ts=ts),
        grid_spec=pltpu.PrefetchScalarGridSpec(
            num_scalar_prefetch=1,
            grid=(bsz, n_exp),
            in_specs=[pl.BlockSpec((1, d, n), lambda b, e, off: (b, 0, 0)),
                      pl.BlockSpec((1, 1, 1, n), lambda b, e, off: (b, e, 0, 0)),
                      pl.BlockSpec((1, d, ff), lambda b, e, off: (e, 0, 0)),
                      pl.BlockSpec((1, d, ff), lambda b, e, off: (e, 0, 0)),
                      pl.BlockSpec((1, ff, d), lambda b, e, off: (e, 0, 0))],
            out_specs=pl.BlockSpec((1, 1, d, cap), lambda b, e, off: (b, e, 0, 0)),
            scratch_shapes=[pltpu.VMEM((d, cap), F32)] + _pair_table_scratch(n, tt, cap, ts)),
        out_shape=jax.ShapeDtypeStruct((bsz, n_exp, d, cap), BF16),
        compiler_params=_cparams("parallel", "arbitrary"),
        name="ffn",
    )(off, h2t, slot4, w_gate, w_up, w_down)


def _comb_kernel(off_ref, out_ref, slot_ref, p_ref, y_ref, pj_ref, pi_ref, *, cap, tt, ts):
    @pl.when(pl.program_id(1) == 0)
    def _():
        y_ref[...] = jnp.zeros_like(y_ref)

    for cols, tile, onehot in _band_pairs(off_ref, slot_ref, pj_ref, pi_ref,
                                          y_ref.shape[2] // tt, tt, ts, cap):
        part = jnp.dot(out_ref[0, 0, :, tile], onehot, preferred_element_type=F32)
        y_ref[0, :, cols] = y_ref[0, :, cols] + part * p_ref[0, 0, :, cols]


def _combine(off, out_t, slot4, probs4, n, tt, ts):
    bsz, n_exp, d, cap = out_t.shape
    row = pl.BlockSpec((1, 1, 1, n), lambda b, e, off: (b, e, 0, 0))
    return pl.pallas_call(
        functools.partial(_comb_kernel, cap=cap, tt=tt, ts=ts),
        grid_spec=pltpu.PrefetchScalarGridSpec(
            num_scalar_prefetch=1,
            grid=(bsz, n_exp),
            in_specs=[pl.BlockSpec((1, 1, d, cap), lambda b, e, off: (b, e, 0, 0)), row, row],
            out_specs=pl.BlockSpec((1, d, n), lambda b, e, off: (b, 0, 0)),
            scratch_shapes=_pair_table_scratch(n, tt, cap, ts)),
        out_shape=jax.ShapeDtypeStruct((bsz, d, n), F32),
        compiler_params=_cparams("parallel", "arbitrary"),
        name="comb",
    )(off, out_t, slot4, probs4)


def _final_kernel(x1_ref, yt_ref, mod_ref, g_ref, o_ref):
    x2 = x1_ref[0] + mod_ref[0, 5:6, :] * yt_ref[0].T
    o_ref[0] = _rms(x2, g_ref[...])


def _final(x1, y_t, mod3, final_g, tn):
    bsz, n, d = x1.shape
    tok = lambda b, j: (b, j, 0)
    return pl.pallas_call(
        _final_kernel,
        grid=(bsz, n // tn),
        in_specs=[pl.BlockSpec((1, tn, d), tok),
                  pl.BlockSpec((1, d, tn), lambda b, j: (b, 0, j)),
                  pl.BlockSpec((1, N_MOD, d), lambda b, j: (b, 0, 0)),
                  pl.BlockSpec((1, d), lambda b, j: (0, 0))],
        out_specs=pl.BlockSpec((1, tn, d), tok),
        out_shape=jax.ShapeDtypeStruct((bsz, n, d), F32),
        compiler_params=_cparams("parallel", "parallel"),
        name="final",
    )(x1, y_t, mod3, final_g)


def kernel(x, c, ctx, c_ctx, ada_w, ada_b, norm1_g, w_in, conv_taps, conv_bias, conv_ln_g,
           conv_ln_b, hgrn_lb_logits, hgrn_norm_g, w_out, norm2_g, router_w, w_gate, w_up,
           w_down, final_g):
    bsz, n, d = x.shape
    assert ada_w.shape[0] == 1, "single-layer stack"
    assert n % GRID_W == 0 and n % CHUNK == 0 and ctx.shape[1] % SUBLANES == 0
    cap = CAPACITY_FACTOR * n // N_EXPERTS
    tn = min(512, n)

    rows = -(-(bsz + 1) // SUBLANES) * SUBLANES
    cvec = jnp.zeros((rows, d), F32).at[:bsz].set(c).at[rows - 1].set(c_ctx)
    mod3 = _modulation(cvec, ada_w[0], ada_b[0]).reshape(rows, N_MOD, d)

    w_in_b = w_in[0].astype(BF16)
    g1 = norm1_g[0].reshape(1, d)
    states = _ctx_states(ctx, mod3, g1, w_in_b[:, OFF_I:OFF_G], hgrn_lb_logits)

    u, q, v, lff, lfb, gs = _project(x, mod3, g1, w_in_b, hgrn_lb_logits, min(2 * tn, n))
    conv_out = _conv_module(u, conv_taps[0], conv_bias[0], conv_ln_g[0], conv_ln_b[0])
    hg = _hgrn_scan(q, v, lff, lfb, gs, states, hgrn_norm_g[0])

    x1, h2t, probs = _out_project(conv_out, hg, x, mod3, w_out[0].astype(BF16),
                                  norm2_g[0].reshape(1, d), router_w[0].T, tn)
    tt = min(2 * MXU_DIM, n)
    ts = min(MXU_DIM, cap)
    slot, off = _expert_choice(probs, cap, tt)
    off = off[:, :, :n // tt + 1].reshape(-1)
    slot4 = slot.reshape(bsz, N_EXPERTS, 1, n)
    probs4 = probs.reshape(bsz, N_EXPERTS, 1, n)
    out_t = _expert_ffn(off, h2t, slot4, w_gate[0].astype(BF16), w_up[0].astype(BF16),
                        w_down[0].astype(BF16), cap, tt, ts)
    y_t = _combine(off, out_t, slot4, probs4, n, tt, ts)
    return _final(x1, y_t, mod3, final_g.reshape(1, d), tn)
```

```python
import functools
import math

import jax
import jax.numpy as jnp
import numpy as np
from jax import lax
from jax.experimental import pallas as pl
from jax.experimental.pallas import tpu as pltpu

F32 = jnp.float32
BF16 = jnp.bfloat16
HIGHEST = lax.Precision.HIGHEST

CONV_WIDTH = 512
HGRN_WIDTH = 512
HEAD_DIM = 128
N_HEADS = HGRN_WIDTH // HEAD_DIM
CONV_K = 31
CONV_PAD = (CONV_K - 1) // 2
GRID_W = 64
CHUNK = 64
N_EXPERTS = 16
CAPACITY_FACTOR = 2
N_MOD = 6
EPS = 1e-6
OFF_CONV_B = CONV_WIDTH
OFF_Q = 2 * CONV_WIDTH
OFF_I = OFF_Q + HGRN_WIDTH
OFF_FF = OFF_I + HGRN_WIDTH
OFF_FB = OFF_FF + HGRN_WIDTH
OFF_G = OFF_FB + HGRN_WIDTH
IN_COLS = OFF_G + HGRN_WIDTH

SUBLANES = 8
LANES = 128
MXU_DIM = 256
VMEM_LIMIT = 56 * 1024 * 1024

N_LEVELS = 6


def _cparams(*sem, flags=None):
    return pltpu.CompilerParams(dimension_semantics=sem, vmem_limit_bytes=VMEM_LIMIT,
                                flags=flags)


def _sigmoid(x):
    return 1.0 / (1.0 + jnp.exp(-x))


def _silu(x):
    return x * _sigmoid(x)


def _rms(x, g):
    return x * lax.rsqrt(jnp.mean(x * x, axis=-1, keepdims=True) + EPS) * g


def _dot_nt(a, b):
    return lax.dot_general(a, b, (((1,), (1,)), ((), ())), preferred_element_type=F32)


def _dot_tn(a, b):
    return lax.dot_general(a, b, (((0,), (0,)), ((), ())), preferred_element_type=F32)


def _dot3(m_bf16, g):
    g0 = g.astype(BF16)
    r1 = g - g0.astype(F32)
    g1 = r1.astype(BF16)
    g2 = (r1 - g1.astype(F32)).astype(BF16)
    acc = jnp.dot(m_bf16, g0, preferred_element_type=F32)
    acc = acc + jnp.dot(m_bf16, g1, preferred_element_type=F32)
    return acc + jnp.dot(m_bf16, g2, preferred_element_type=F32)


def _lower_bound(lbl_ref, direction):
    l = lbl_ref[:, direction:direction + 1, :]
    m = jnp.max(l, axis=0)
    e = jnp.exp(l - m[None])
    return e[0] / jnp.sum(e, axis=0)


def _mod_kernel(cv_ref, w_ref, b_ref, o_ref):
    s = _silu(cv_ref[...])
    o_ref[...] = jnp.dot(s, w_ref[...], precision=HIGHEST,
                         preferred_element_type=F32) + b_ref[...]


def _modulation(cvec, w, b):
    rows, d = cvec.shape
    cols = w.shape[1]
    return pl.pallas_call(
        _mod_kernel,
        grid=(cols // d,),
        in_specs=[pl.BlockSpec((rows, d), lambda j: (0, 0)),
                  pl.BlockSpec((d, d), lambda j: (0, j)),
                  pl.BlockSpec((1, d), lambda j: (0, j))],
        out_specs=pl.BlockSpec((rows, d), lambda j: (0, j)),
        out_shape=jax.ShapeDtypeStruct((rows, cols), F32),
        compiler_params=_cparams("parallel"),
        name="mod",
    )(cvec, w, b.reshape(1, cols))


def _ctx_kernel(ctx_ref, mod_ref, g_ref, w_ref, lbl_ref, tri_ref, o_ref):
    x = ctx_ref[0]
    h = _rms(x, g_ref[...]) * (1.0 + mod_ref[0, 1:2, :]) + mod_ref[0, 0:1, :]
    p = jnp.dot(h.astype(BF16), w_ref[...], preferred_element_type=F32)
    v = p[:, :HGRN_WIDTH].astype(BF16)
    for direction in range(2):
        z = p[:, (1 + direction) * HGRN_WIDTH:(2 + direction) * HGRN_WIDTH]
        lb = _lower_bound(lbl_ref, direction)
        f = lb + (1.0 - lb) * _sigmoid(z)
        expo = _dot3(tri_ref[direction], jnp.log(f))
        kd = ((1.0 - f) * jnp.exp(expo)).astype(BF16)
        for hd in range(N_HEADS):
            sl = slice(hd * HEAD_DIM, (hd + 1) * HEAD_DIM)
            o_ref[0, direction, hd] = _dot_tn(v[:, sl], kd[:, sl])


def _ctx_states(ctx, mod3, norm_g, w_ctx, lb_logits):
    bsz, length, d = ctx.shape
    idx = np.arange(length)
    tri = np.stack([idx[None, :] > idx[:, None], idx[None, :] < idx[:, None]])
    tri = jnp.asarray(tri.astype(np.float32), dtype=BF16)
    ctx_row = mod3.shape[0] - 1
    return pl.pallas_call(
        _ctx_kernel,
        grid=(bsz,),
        in_specs=[pl.BlockSpec((1, length, d), lambda b: (b, 0, 0)),
                  pl.BlockSpec((1, N_MOD, d), lambda b: (ctx_row, 0, 0)),
                  pl.BlockSpec((1, d), lambda b: (0, 0)),
                  pl.BlockSpec(w_ctx.shape, lambda b: (0, 0)),
                  pl.BlockSpec(lb_logits.shape, lambda b: (0, 0, 0)),
                  pl.BlockSpec(tri.shape, lambda b: (0, 0, 0))],
        out_specs=pl.BlockSpec((1, 2, N_HEADS, HEAD_DIM, HEAD_DIM), lambda b: (b, 0, 0, 0, 0)),
        out_shape=jax.ShapeDtypeStruct((bsz, 2, N_HEADS, HEAD_DIM, HEAD_DIM), F32),
        compiler_params=_cparams("parallel"),
        name="ctx",
    )(ctx, mod3, norm_g, w_ctx, lb_logits, tri)


def _proj_kernel(x_ref, mod_ref, g_ref, w_ref, lbl_ref,
                 u_ref, q_ref, v_ref, lff_ref, lfb_ref, gs_ref):
    x = x_ref[0]
    h = _rms(x, g_ref[...]) * (1.0 + mod_ref[0, 1:2, :]) + mod_ref[0, 0:1, :]
    hb = h.astype(BF16)

    def cols(off, width):
        return jnp.dot(hb, w_ref[:, off:off + width], preferred_element_type=F32)

    u_ref[0] = (cols(0, CONV_WIDTH) * _sigmoid(cols(OFF_CONV_B, CONV_WIDTH))).astype(BF16)
    for direction, (off, out) in enumerate(((OFF_FF, lff_ref), (OFF_FB, lfb_ref))):
        lb = _lower_bound(lbl_ref, direction)
        out[0] = jnp.log2(lb + (1.0 - lb) * _sigmoid(cols(off, HGRN_WIDTH)))
    gs_ref[0] = _silu(cols(OFF_G, HGRN_WIDTH)).astype(BF16)
    q_ref[0] = cols(OFF_Q, HGRN_WIDTH).astype(BF16)
    v_ref[0] = cols(OFF_I, HGRN_WIDTH).astype(BF16)


def _project(x, mod3, norm_g, w_in, lb_logits, tn):
    bsz, n, d = x.shape
    tok = lambda b, j: (b, j, 0)
    out_block = pl.BlockSpec((1, tn, HGRN_WIDTH), tok)
    sds = lambda dt: jax.ShapeDtypeStruct((bsz, n, HGRN_WIDTH), dt)
    return pl.pallas_call(
        _proj_kernel,
        grid=(bsz, n // tn),
        in_specs=[pl.BlockSpec((1, tn, d), tok),
                  pl.BlockSpec((1, N_MOD, d), lambda b, j: (b, 0, 0)),
                  pl.BlockSpec((1, d), lambda b, j: (0, 0)),
                  pl.BlockSpec(w_in.shape, lambda b, j: (0, 0)),
                  pl.BlockSpec(lb_logits.shape, lambda b, j: (0, 0, 0))],
        out_specs=[out_block] * 6,
        out_shape=[sds(BF16), sds(BF16), sds(BF16), sds(F32), sds(F32), sds(BF16)],
        compiler_params=_cparams("parallel", "parallel"),
        name="proj",
    )(x, mod3, norm_g, w_in, lb_logits)


W_OFF = 16
W_PADDED = GRID_W + 2 * W_OFF


def _conv_kernel(u_ref, taps_ref, bias_ref, lng_ref, lnb_ref, o_ref, hbuf, vbuf):
    rows = u_ref.shape[1]
    half = CONV_WIDTH // 2
    zeros_h = jnp.zeros((rows, W_OFF, half), F32)
    hbuf[:, 0:W_OFF, :] = zeros_h
    hbuf[:, W_OFF + GRID_W:W_PADDED, :] = zeros_h
    hbuf[:, W_OFF:W_OFF + GRID_W, :] = u_ref[0, :, :, 0:half].astype(F32)
    zeros_v = jnp.zeros((W_OFF, GRID_W, half), F32)
    vbuf[0:W_OFF] = zeros_v
    vbuf[W_OFF + rows:W_OFF + rows + W_OFF] = zeros_v
    vbuf[W_OFF:W_OFF + rows] = u_ref[0, :, :, half:CONV_WIDTH].astype(F32)

    span = GRID_W + SUBLANES

    def scaled(x, k, lo):
        t = taps_ref[k, :, lo:lo + half]
        return (x.reshape(-1, SUBLANES, half) * t[None]).reshape(x.shape)

    def conv_row(r):
        acc_h = jnp.zeros((GRID_W, half), F32)
        for b in range(SUBLANES):
            part = None
            for k in range(b - 1, CONV_K, SUBLANES):
                if k < 0:
                    continue
                start = k + 1 - b
                term = scaled(hbuf[r, start:start + span, :], k, 0)
                part = term if part is None else part + term
            acc_h = acc_h + part[b:b + GRID_W]
        acc_v = jnp.zeros((GRID_W, half), F32)
        for k in range(CONV_K):
            acc_v = acc_v + scaled(vbuf[r + W_OFF + k - CONV_PAD], k, half)
        return acc_h + bias_ref[:, 0:half], acc_v + bias_ref[:, half:CONV_WIDTH]

    def norm_row(r, acc_h, acc_v):
        mu = (jnp.sum(acc_h, axis=-1, keepdims=True)
              + jnp.sum(acc_v, axis=-1, keepdims=True)) / CONV_WIDTH
        ch = acc_h - mu
        cv = acc_v - mu
        var = (jnp.sum(ch * ch, axis=-1, keepdims=True)
               + jnp.sum(cv * cv, axis=-1, keepdims=True)) / CONV_WIDTH
        rs = lax.rsqrt(var + EPS)
        yh = ch * rs * lng_ref[:, 0:half] + lnb_ref[:, 0:half]
        yv = cv * rs * lng_ref[:, half:CONV_WIDTH] + lnb_ref[:, half:CONV_WIDTH]
        o_ref[0, r, :, 0:half] = _silu(yh).astype(BF16)
        o_ref[0, r, :, half:CONV_WIDTH] = _silu(yv).astype(BF16)

    def row_pair(i, carry):
        first = conv_row(2 * i)
        second = conv_row(2 * i + 1)
        norm_row(2 * i, *first)
        norm_row(2 * i + 1, *second)
        return carry

    lax.fori_loop(0, rows // 2, row_pair, 0)


def _conv_module(u, taps, bias, ln_g, ln_b):
    bsz, n, cw = u.shape
    rows = n // GRID_W
    half = cw // 2
    assert rows % 2 == 0
    u4 = u.reshape(bsz, rows, GRID_W, cw)
    vec = lambda a: a.reshape(1, cw)
    taps = jnp.broadcast_to(taps[:, None, :], (CONV_K, SUBLANES, cw))
    out = pl.pallas_call(
        _conv_kernel,
        grid=(bsz,),
        in_specs=[pl.BlockSpec((1, rows, GRID_W, cw), lambda b: (b, 0, 0, 0)),
                  pl.BlockSpec((CONV_K, SUBLANES, cw), lambda b: (0, 0, 0)),
                  pl.BlockSpec((1, cw), lambda b: (0, 0)),
                  pl.BlockSpec((1, cw), lambda b: (0, 0)),
                  pl.BlockSpec((1, cw), lambda b: (0, 0))],
        out_specs=pl.BlockSpec((1, rows, GRID_W, cw), lambda b: (b, 0, 0, 0)),
        out_shape=jax.ShapeDtypeStruct((bsz, rows, GRID_W, cw), BF16),
        scratch_shapes=[pltpu.VMEM((rows, W_PADDED, half), F32),
                        pltpu.VMEM((rows + 2 * W_OFF, GRID_W, half), F32)],
        compiler_params=_cparams("parallel"),
        name="conv",
    )(u4, taps, vec(bias), vec(ln_g), vec(ln_b))
    return out.reshape(bsz, n, cw)


def _scan_constants():
    t = np.arange(CHUNK)
    cum_f = (t[None, :] <= t[:, None]).astype(np.float32)
    signs, masks = [], []
    for level in range(N_LEVELS):
        s = CHUNK >> (level + 1)
        upper = (t // s) % 2 == 1
        same = (t[:, None] // (2 * s)) == (t[None, :] // (2 * s))
        masks.append(same & upper[:, None] & ~upper[None, :])
        signs.append(np.where(upper, 1.0, -1.0))
    masks.append(t[:, None] == t[None, :])
    sign_f = np.broadcast_to(np.stack(signs[:-1])[:, :, None], (N_LEVELS - 1, CHUNK, HEAD_DIM))
    pair_f = np.stack(masks).astype(np.float32)
    cum = np.stack([np.tile(m, (1, 3)) for m in (cum_f, cum_f[::-1, ::-1])])
    sign = np.stack([sign_f, sign_f[:, ::-1]])
    pair = np.stack([pair_f, pair_f[:, ::-1, ::-1]])
    return (jnp.asarray(cum, dtype=BF16), jnp.asarray(sign, dtype=F32),
            jnp.asarray(pair, dtype=F32))


def _level_factors(x, f, sign_ref, direction):
    dk = x.shape[1]
    factors = []
    for level in range(N_LEVELS - 1):
        s = CHUNK >> (level + 1)
        refs = [jnp.broadcast_to(x[r:r + 1, :], (2 * s, dk))
                for r in range(s - 1 + direction, CHUNK, 2 * s)]
        x_ref = refs[0] if len(refs) == 1 else jnp.concatenate(refs, axis=0)
        factors.append(jnp.exp2((x - x_ref) * sign_ref[direction, level]))
    factors.append(f)
    return factors


def _scan_kernel(q_ref, v_ref, lff_ref, lfb_ref, gs_ref, s0_ref, ng_ref, m_ref, sg_ref, pm_ref,
                 o_ref, acc_ref, qe_ref, inc_ref, dec_ref, st_ref, *, unroll):
    n = q_ref.shape[1]
    n_chunks = n // CHUNK
    lf_refs = (lff_ref, lfb_ref)
    total_rows = (CHUNK - 1, 0)

    def local(i, carry):
        chunks = [i * unroll + u for u in range(unroll)]
        rows = [pl.ds(pl.multiple_of(c * CHUNK, CHUNK), CHUNK) for c in chunks]
        chains = [(u, d) for u in range(unroll) for d in range(2)]
        q = [q_ref[0, r, :].astype(F32) for r in rows]
        qb = [t.astype(BF16) for t in q]
        v = [v_ref[0, r, :] for r in rows]
        g, x = {}, {}
        for u, d in chains:
            g[u, d] = lf_refs[d][0, rows[u], :]
            g0 = g[u, d].astype(BF16)
            r1 = g[u, d] - g0.astype(F32)
            g1 = r1.astype(BF16)
            g2 = (r1 - g1.astype(F32)).astype(BF16)
            x[u, d] = jnp.dot(m_ref[d], jnp.concatenate([g0, g1, g2], axis=0),
                              preferred_element_type=F32)
        kb, ql, kl, kd = {}, {}, {}, {}
        for u, d in chains:
            xc = x[u, d]
            x_tot = xc[total_rows[d]:total_rows[d] + 1, :]
            f = jnp.exp2(g[u, d])
            k = 1.0 - f
            kb[u, d] = k.astype(BF16)
            qe_ref[rows[u], d * HEAD_DIM:(d + 1) * HEAD_DIM] = (q[u] * jnp.exp2(xc)).astype(BF16)
            factors = _level_factors(xc, f, sg_ref, d)
            ql[u, d] = [(q[u] * e).astype(BF16) for e in factors]
            kl[u, d] = [(k * e).astype(BF16) for e in factors[:-1]] + [kb[u, d]]
            kd[u, d] = (k * jnp.exp2(x_tot - xc)).astype(BF16)
            dec_ref[d, chunks[u]] = jnp.broadcast_to(jnp.exp2(x_tot), (SUBLANES, HEAD_DIM))
        scores = {}
        for u, d in chains:
            s_d = _dot_nt(qb[u], kb[u, d])
            scores[u, d] = jnp.where(pm_ref[d, N_LEVELS] > 0.5, s_d, 0.0)
        for level in range(N_LEVELS):
            for u, d in chains:
                s_l = _dot_nt(ql[u, d][level], kl[u, d][level])
                scores[u, d] = jnp.where(pm_ref[d, level] > 0.5, s_l, scores[u, d])
        for u in range(unroll):
            out = jnp.dot(scores[u, 0].astype(BF16), v[u], preferred_element_type=F32)
            out = out + jnp.dot(scores[u, 1].astype(BF16), v[u], preferred_element_type=F32)
            acc_ref[rows[u], :] = out
        for u, d in chains:
            inc_ref[d, chunks[u]] = _dot_tn(v[u], kd[u, d])
        return carry

    lax.fori_loop(0, n_chunks // unroll, local, 0)

    def recur(i, states):
        nxt = []
        for direction, c in ((0, i), (1, n_chunks - 1 - i)):
            st = states[direction]
            st_ref[c, :, direction * HEAD_DIM:(direction + 1) * HEAD_DIM] = st.astype(BF16)
            nxt.append(st * dec_ref[direction, c, 0:1, :] + inc_ref[direction, c])
        return tuple(nxt)

    lax.fori_loop(0, n_chunks, recur, (s0_ref[0, 0, 0], s0_ref[0, 1, 0]))

    wide = math.gcd(n_chunks, 32)

    def readout(i, carry):
        chunks = [i * wide + u for u in range(wide)]
        rows = [pl.ds(pl.multiple_of(c * CHUNK, CHUNK), CHUNK) for c in chunks]
        inter = [_dot_nt(qe_ref[rows[u], :], st_ref[chunks[u]]) for u in range(wide)]
        for u in range(wide):
            o = acc_ref[rows[u], :] + inter[u]
            y = o * lax.rsqrt(jnp.mean(o * o, axis=-1, keepdims=True) + EPS)
            o_ref[0, rows[u], :] = (y * ng_ref[...]
                                    * gs_ref[0, rows[u], :].astype(F32)).astype(BF16)
        return carry

    lax.fori_loop(0, n_chunks // wide, readout, 0)


def _hgrn_scan(q, v, lff, lfb, gs, states, norm_g):
    bsz, n, width = q.shape
    n_chunks = n // CHUNK
    m_const, sg_const, pm_const = _scan_constants()
    head = pl.BlockSpec((1, n, HEAD_DIM), lambda b, h: (b, 0, h))
    return pl.pallas_call(
        functools.partial(_scan_kernel, unroll=math.gcd(n_chunks, 16)),
        grid=(bsz, N_HEADS),
        in_specs=[head, head, head, head, head,
                  pl.BlockSpec((1, 2, 1, HEAD_DIM, HEAD_DIM), lambda b, h: (b, 0, h, 0, 0)),
                  pl.BlockSpec((1, HEAD_DIM), lambda b, h: (0, h)),
                  pl.BlockSpec(m_const.shape, lambda b, h: (0, 0, 0)),
                  pl.BlockSpec(sg_const.shape, lambda b, h: (0, 0, 0, 0)),
                  pl.BlockSpec(pm_const.shape, lambda b, h: (0, 0, 0, 0))],
        out_specs=head,
        out_shape=jax.ShapeDtypeStruct((bsz, n, width), BF16),
        scratch_shapes=[pltpu.VMEM((n, HEAD_DIM), F32),
                        pltpu.VMEM((n, 2 * HEAD_DIM), BF16),
                        pltpu.VMEM((2, n_chunks, HEAD_DIM, HEAD_DIM), F32),
                        pltpu.VMEM((2, n_chunks, SUBLANES, HEAD_DIM), F32),
                        pltpu.VMEM((n_chunks, HEAD_DIM, 2 * HEAD_DIM), BF16)],
        compiler_params=_cparams("parallel", "parallel"),
        name="scan",
    )(q, v, lff, lfb, gs, states, norm_g.reshape(1, width), m_const, sg_const, pm_const)


def _oproj_kernel(cv_ref, hg_ref, x_ref, mod_ref, w_ref, g2_ref, rw_ref,
                  x1_ref, h2t_ref, pr_ref):
    mix = jnp.dot(cv_ref[0], w_ref[0:CONV_WIDTH, :], preferred_element_type=F32)
    mix = mix + jnp.dot(hg_ref[0], w_ref[CONV_WIDTH:, :], preferred_element_type=F32)
    x1 = x_ref[0] + mod_ref[0, 2:3, :] * mix
    x1_ref[0] = x1
    h2 = _rms(x1, g2_ref[...]) * (1.0 + mod_ref[0, 4:5, :]) + mod_ref[0, 3:4, :]
    h2t_ref[0] = h2.T.astype(BF16)
    h_hi = h2.astype(BF16)
    h_lo = (h2 - h_hi.astype(F32)).astype(BF16)
    r_hi = rw_ref[...].astype(BF16)
    r_lo = (rw_ref[...] - r_hi.astype(F32)).astype(BF16)
    logits = _dot_nt(r_hi, h_hi) + _dot_nt(r_lo, h_hi) + _dot_nt(r_hi, h_lo)
    z = jnp.exp(logits - jnp.max(logits, axis=0, keepdims=True))
    pr_ref[0] = z / jnp.sum(z, axis=0, keepdims=True)


def _out_project(conv_out, hg, x, mod3, w_out, norm2_g, router_t, tn):
    bsz, n, d = x.shape
    tok = lambda b, j: (b, j, 0)
    return pl.pallas_call(
        _oproj_kernel,
        grid=(bsz, n // tn),
        in_specs=[pl.BlockSpec((1, tn, CONV_WIDTH), tok),
                  pl.BlockSpec((1, tn, HGRN_WIDTH), tok),
                  pl.BlockSpec((1, tn, d), tok),
                  pl.BlockSpec((1, N_MOD, d), lambda b, j: (b, 0, 0)),
                  pl.BlockSpec(w_out.shape, lambda b, j: (0, 0)),
                  pl.BlockSpec((1, d), lambda b, j: (0, 0)),
                  pl.BlockSpec(router_t.shape, lambda b, j: (0, 0))],
        out_specs=[pl.BlockSpec((1, tn, d), tok),
                   pl.BlockSpec((1, d, tn), lambda b, j: (b, 0, j)),
                   pl.BlockSpec((1, N_EXPERTS, tn), lambda b, j: (b, 0, j))],
        out_shape=[jax.ShapeDtypeStruct((bsz, n, d), F32),
                   jax.ShapeDtypeStruct((bsz, d, n), BF16),
                   jax.ShapeDtypeStruct((bsz, N_EXPERTS, n), F32)],
        compiler_params=_cparams("parallel", "parallel"),
        name="oproj",
    )(conv_out, hg, x, mod3, w_out, norm2_g, router_t)


def _prefix_count(flags, tri_ref):
    n = flags.shape[1]
    run = jnp.zeros((flags.shape[0], 1), F32)
    pieces = []
    for j in range(n // LANES):
        blk = flags[:, j * LANES:(j + 1) * LANES]
        inc = jnp.dot(blk.astype(BF16), tri_ref[...], preferred_element_type=F32)
        pieces.append(inc - blk + run)
        run = run + inc[:, LANES - 1:LANES]
    return jnp.concatenate(pieces, axis=1)


def _topk_kernel(p_ref, tri_ref, tile_ref, slot_ref, off_ref, *, cap):
    p = p_ref[0]
    n_exp = p.shape[0]
    capf = jnp.float32(cap)

    def count_ge(t):
        return jnp.sum(jnp.where(p >= t, 1.0, 0.0), axis=1, keepdims=True)

    def cond(state):
        return state[2] > 0

    def body(state):
        lo, hi, _ = state
        mid = 0.5 * (lo + hi)
        ge = count_ge(mid) >= capf
        lo_n = jnp.where(ge, mid, lo)
        hi_n = jnp.where(ge, hi, mid)
        nxt = 0.5 * (lo_n + hi_n)
        open_ = jnp.where((nxt > lo_n) & (nxt < hi_n), 1, 0)
        return lo_n, hi_n, jnp.max(open_)

    lo0 = jnp.zeros((n_exp, 1), p.dtype)
    hi0 = jnp.full((n_exp, 1), 2.0, p.dtype)
    thr, _, _ = lax.while_loop(cond, body, (lo0, hi0, jnp.int32(1)))
    gt = jnp.where(p > thr, 1.0, 0.0)
    eq = jnp.where(p == thr, 1.0, 0.0)
    need = capf - jnp.sum(gt, axis=1, keepdims=True)
    sel = gt + eq * jnp.where(_prefix_count(eq, tri_ref) < need, 1.0, 0.0)
    slot = _prefix_count(sel, tri_ref)
    slot_ref[0] = jnp.where(sel > 0.5, slot, -1.0).astype(jnp.int32)
    off_ref[0] = jnp.dot(sel.astype(BF16), tile_ref[...],
                         preferred_element_type=F32).astype(jnp.int32)


def _expert_choice(probs, cap, tt):
    bsz, n_exp, n = probs.shape
    idx = np.arange(LANES)
    tri = jnp.asarray((idx[:, None] <= idx[None, :]).astype(np.float32), dtype=BF16)
    assert n // tt < LANES
    tile = jnp.asarray((np.arange(n)[:, None] < idx[None, :] * tt).astype(np.float32), dtype=BF16)
    rows = bsz * n_exp
    slot, off = pl.pallas_call(
        functools.partial(_topk_kernel, cap=cap),
        grid=(1,),
        in_specs=[pl.BlockSpec((1, rows, n), lambda b: (0, 0, 0)),
                  pl.BlockSpec((LANES, LANES), lambda b: (0, 0)),
                  pl.BlockSpec((n, LANES), lambda b: (0, 0))],
        out_specs=[pl.BlockSpec((1, rows, n), lambda b: (0, 0, 0)),
                   pl.BlockSpec((1, rows, LANES), lambda b: (0, 0, 0))],
        out_shape=[jax.ShapeDtypeStruct((1, rows, n), jnp.int32),
                   jax.ShapeDtypeStruct((1, rows, LANES), jnp.int32)],
        compiler_params=_cparams("arbitrary"),
        name="topk",
    )(probs.reshape(1, rows, n), tri, tile)
    return slot.reshape(bsz, n_exp, n), off.reshape(bsz, n_exp, LANES)


def _one_hot_rows(slot_row, first, rows):
    ids = lax.broadcasted_iota(jnp.int32, (rows, slot_row.shape[1]), 0) + first
    return jnp.where(ids == slot_row, 1.0, 0.0).astype(BF16)


def _pair_table_scratch(n, tt, cap, ts):
    n_pairs = n // tt + cap // ts - 1
    return [pltpu.SMEM((n_pairs,), jnp.int32), pltpu.SMEM((n_pairs,), jnp.int32)]


def _band_pairs(off_ref, slot_ref, pj_ref, pi_ref, n_tt, tt, ts, cap):
    n_st = cap // ts
    n_pairs = n_tt + n_st - 1
    base = (pl.program_id(0) * pl.num_programs(1) + pl.program_id(1)) * (n_tt + 1)
    for p in range(n_pairs):
        pj_ref[p] = 0
        pi_ref[p] = n_st
    count = jnp.int32(0)
    for i in range(n_st):
        for j in range(n_tt):
            lo = off_ref[base + j]
            hi = off_ref[base + j + 1]
            hit = (hi > lo) & (hi > i * ts) & (lo < (i + 1) * ts)

            @pl.when(hit)
            def _(i=i, j=j, count=count):
                pj_ref[count] = j
                pi_ref[count] = i

            count = count + hit.astype(jnp.int32)
    pairs = []
    for p in range(n_pairs):
        i = pi_ref[p]
        cols = pl.ds(pl.multiple_of(pj_ref[p] * tt, tt), tt)
        tile = pl.ds(pl.multiple_of(jnp.minimum(i, n_st - 1) * ts, ts), ts)
        pairs.append((cols, tile, _one_hot_rows(slot_ref[0, 0, :, cols], i * ts, ts)))
    return pairs


def _ffn_kernel(off_ref, ht_ref, slot_ref, wg_ref, wu_ref, wd_ref, o_ref, xs_ref,
                pj_ref, pi_ref, *, cap, tt, ts):
    xs_ref[...] = jnp.zeros_like(xs_ref)
    for cols, tile, onehot in _band_pairs(off_ref, slot_ref, pj_ref, pi_ref,
                                          ht_ref.shape[2] // tt, tt, ts, cap):
        xs_ref[:, tile] += _dot_nt(ht_ref[0, :, cols], onehot)
    xs = xs_ref[...].astype(BF16)
    gate_t = _dot_tn(wg_ref[0], xs)
    up_t = _dot_tn(wu_ref[0], xs)
    hid_t = (_silu(gate_t) * up_t).astype(BF16)
    o_ref[0, 0] = _dot_tn(wd_ref[0], hid_t).astype(BF16)


def _expert_ffn(off, h2t, slot4, w_gate, w_up, w_down, cap, tt, ts):
    bsz, d, n = h2t.shape
    n_exp, _, ff = w_gate.shape
    return pl.pallas_call(
        functools.partial(_ffn_kernel, cap=cap, tt=tt, ts=ts),
        grid_spec=pltpu.PrefetchScalarGridSpec(
            num_scalar_prefetch=1,
            grid=(bsz, n_exp),
            in_specs=[pl.BlockSpec((1, d, n), lambda b, e, off: (b, 0, 0)),
                      pl.BlockSpec((1, 1, 1, n), lambda b, e, off: (b, e, 0, 0)),
                      pl.BlockSpec((1, d, ff), lambda b, e, off: (e, 0, 0)),
                      pl.BlockSpec((1, d, ff), lambda b, e, off: (e, 0, 0)),
                      pl.BlockSpec((1, ff, d), lambda b, e, off: (e, 0, 0))],
            out_specs=pl.BlockSpec((1, 1, d, cap), lambda b, e, off: (b, e, 0, 0)),
            scratch_shapes=[pltpu.VMEM((d, cap), F32)] + _pair_table_scratch(n, tt, cap, ts)),
        out_shape=jax.ShapeDtypeStruct((bsz, n_exp, d, cap), BF16),
        compiler_params=_cparams("parallel", "arbitrary"),
        name="ffn",
    )(off, h2t, slot4, w_gate, w_up, w_down)


def _comb_kernel(off_ref, out_ref, slot_ref, p_ref, y_ref, pj_ref, pi_ref, *, cap, tt, ts):
    @pl.when(pl.program_id(1) == 0)
    def _():
        y_ref[...] = jnp.zeros_like(y_ref)

    for cols, tile, onehot in _band_pairs(off_ref, slot_ref, pj_ref, pi_ref,
                                          y_ref.shape[2] // tt, tt, ts, cap):
        part = jnp.dot(out_ref[0, 0, :, tile], onehot, preferred_element_type=F32)
        y_ref[0, :, cols] = y_ref[0, :, cols] + part * p_ref[0, 0, :, cols]


def _combine(off, out_t, slot4, probs4, n, tt, ts):
    bsz, n_exp, d, cap = out_t.shape
    row = pl.BlockSpec((1, 1, 1, n), lambda b, e, off: (b, e, 0, 0))
    return pl.pallas_call(
        functools.partial(_comb_kernel, cap=cap, tt=tt, ts=ts),
        grid_spec=pltpu.PrefetchScalarGridSpec(
            num_scalar_prefetch=1,
            grid=(bsz, n_exp),
            in_specs=[pl.BlockSpec((1, 1, d, cap), lambda b, e, off: (b, e, 0, 0)), row, row],
            out_specs=pl.BlockSpec((1, d, n), lambda b, e, off: (b, 0, 0)),
            scratch_shapes=_pair_table_scratch(n, tt, cap, ts)),
        out_shape=jax.ShapeDtypeStruct((bsz, d, n), F32),
        compiler_params=_cparams("parallel", "arbitrary"),
        name="comb",
    )(off, out_t, slot4, probs4)


def _final_kernel(x1_ref, yt_ref, mod_ref, g_ref, o_ref):
    x2 = x1_ref[0] + mod_ref[0, 5:6, :] * yt_ref[0].T
    o_ref[0] = _rms(x2, g_ref[...])


def _final(x1, y_t, mod3, final_g, tn):
    bsz, n, d = x1.shape
    tok = lambda b, j: (b, j, 0)
    return pl.pallas_call(
        _final_kernel,
        grid=(bsz, n // tn),
        in_specs=[pl.BlockSpec((1, tn, d), tok),
                  pl.BlockSpec((1, d, tn), lambda b, j: (b, 0, j)),
                  pl.BlockSpec((1, N_MOD, d), lambda b, j: (b, 0, 0)),
                  pl.BlockSpec((1, d), lambda b, j: (0, 0))],
        out_specs=pl.BlockSpec((1, tn, d), tok),
        out_shape=jax.ShapeDtypeStruct((bsz, n, d), F32),
        compiler_params=_cparams("parallel", "parallel"),
        name="final",
    )(x1, y_t, mod3, final_g)


def kernel(x, c, ctx, c_ctx, ada_w, ada_b, norm1_g, w_in, conv_taps, conv_bias, conv_ln_g,
           conv_ln_b, hgrn_lb_logits, hgrn_norm_g, w_out, norm2_g, router_w, w_gate, w_up,
           w_down, final_g):
    bsz, n, d = x.shape
    assert ada_w.shape[0] == 1, "single-layer stack"
    assert n % GRID_W == 0 and n % CHUNK == 0 and ctx.shape[1] % SUBLANES == 0
    cap = CAPACITY_FACTOR * n // N_EXPERTS
    tn = min(512, n)

    rows = -(-(bsz + 1) // SUBLANES) * SUBLANES
    cvec = jnp.zeros((rows, d), F32).at[:bsz].set(c).at[rows - 1].set(c_ctx)
    mod3 = _modulation(cvec, ada_w[0], ada_b[0]).reshape(rows, N_MOD, d)

    w_in_b = w_in[0].astype(BF16)
    g1 = norm1_g[0].reshape(1, d)
    states = _ctx_states(ctx, mod3, g1, w_in_b[:, OFF_I:OFF_G], hgrn_lb_logits)

    u, q, v, lff, lfb, gs = _project(x, mod3, g1, w_in_b, hgrn_lb_logits, min(2 * tn, n))
    conv_out = _conv_module(u, conv_taps[0], conv_bias[0], conv_ln_g[0], conv_ln_b[0])
    hg = _hgrn_scan(q, v, lff, lfb, gs, states, hgrn_norm_g[0])

    x1, h2t, probs = _out_project(conv_out, hg, x, mod3, w_out[0].astype(BF16),
                                  norm2_g[0].reshape(1, d), router_w[0].T, tn)
    tt = min(2 * MXU_DIM, n)
    ts = min(MXU_DIM, cap)
    slot, off = _expert_choice(probs, cap, tt)
    off = off[:, :, :n // tt + 1].reshape(-1)
    slot4 = slot.reshape(bsz, N_EXPERTS, 1, n)
    probs4 = probs.reshape(bsz, N_EXPERTS, 1, n)
    out_t = _expert_ffn(off, h2t, slot4, w_gate[0].astype(BF16), w_up[0].astype(BF16),
                        w_down[0].astype(BF16), cap, tt, ts)
    y_t = _combine(off, out_t, slot4, probs4, n, tt, ts)
    return _final(x1, y_t, mod3, final_g.reshape(1, d), tn)
```
